```python
import math
import jax, jax.numpy as jnp
from jax import lax
import numpy as np

D_MODEL = 1024
BATCH = 4
SEQ = 4096
DEPTH = 1
DEC_BATCH = 128
DEC_SEQ = 4
PAST_LEN = 16384
PAGE_SIZE = 128

N_META = 16
ROPE_THETA = 10000.0
EPS = 1e-6
Q_BLOCK = 128
MLA_HEADS = 8
MLA_NOPE = 64
MLA_ROPE = 32
MLA_VDIM = 64
MLA_Q_RANK = 256
MLA_KV_RANK = 128
MLA_SCALE = (MLA_NOPE + MLA_ROPE) ** -0.5
DIFF_HEADS = 4
DIFF_HD = 64
DIFF_VD = 2 * DIFF_HD
DIFF_SCALE = DIFF_HD ** -0.5
MIX_WIDTH = MLA_HEADS * MLA_VDIM + DIFF_HEADS * DIFF_VD
IN_SIZES = (MLA_Q_RANK, MLA_KV_RANK, MLA_ROPE, DIFF_HEADS * 2 * DIFF_HD, DIFF_HEADS * 2 * DIFF_HD, DIFF_HEADS * DIFF_VD)
IN_DIM = sum(IN_SIZES)
PEER_HEADS = 8
N_KEYS = 128
N_EXPERTS = N_KEYS * N_KEYS
PEER_QDIM = 128
PEER_TOPK = 16
PEER_BLOCK = 128

kernel_name = "hymba_mla_diffattn_peer_step"


def rms_norm(x, g):
    xf = x.astype(jnp.float32)
    y = xf * lax.rsqrt(jnp.mean(xf * xf, axis=-1, keepdims=True) + EPS)
    return (y * g.astype(jnp.float32)).astype(x.dtype)


def rope(x, pos):
    half = x.shape[-1] // 2
    inv = ROPE_THETA ** (-jnp.arange(half, dtype=jnp.float32) / half)
    ang = pos.astype(jnp.float32)[:, None] * inv[None, :]
    ang = ang.reshape((1, pos.shape[0]) + (1,) * (x.ndim - 3) + (half,))
    c, s = jnp.cos(ang), jnp.sin(ang)
    xf = x.astype(jnp.float32)
    x1, x2 = xf[..., :half], xf[..., half:]
    return jnp.concatenate([x1 * c - x2 * s, x1 * s + x2 * c], axis=-1).astype(x.dtype)


def split_dk(dk):
    return dk.reshape(dk.shape[:-1] + (2, DIFF_HD))


def mixer_inputs(h, pos, w_in, g_cq, w_uq, g_ckv, g_qn, g_qr, g_kr, g_dq, g_dk):
    b, p, _ = h.shape
    cuts = [int(c) for c in np.cumsum(IN_SIZES)[:-1]]
    c_q, c_kv, k_r, dq, dk, dv = jnp.split(h @ w_in, cuts, axis=-1)
    q = (rms_norm(c_q, g_cq) @ w_uq).reshape(b, p, MLA_HEADS, MLA_NOPE + MLA_ROPE)
    q_nope = rms_norm(q[..., :MLA_NOPE], g_qn)
    q_rope = rope(rms_norm(q[..., MLA_NOPE:], g_qr), pos)
    c_kv = rms_norm(c_kv, g_ckv)
    k_r = rope(rms_norm(k_r, g_kr), pos)
    dq = rope(rms_norm(dq.reshape(b, p, DIFF_HEADS, 2, DIFF_HD), g_dq), pos)
    dk = rope(rms_norm(dk.reshape(b, p, DIFF_HEADS, 2, DIFF_HD), g_dk), pos)
    dk = dk.reshape(b, p, DIFF_HEADS, 2 * DIFF_HD)
    dv = dv.reshape(b, p, DIFF_HEADS, DIFF_VD)
    return (q_nope, q_rope, dq), (c_kv, k_r, dk, dv)


def mla_expand(c_kv, w_uk, w_uv, g_kn):
    lead = c_kv.shape[:-1]
    k_nope = rms_norm((c_kv @ w_uk).reshape(lead + (MLA_HEADS, MLA_NOPE)), g_kn)
    v = (c_kv @ w_uv).reshape(lead + (MLA_HEADS, MLA_VDIM))
    return k_nope, v


def softmax_stats(s, mask):
    if mask is not None:
        s = jnp.where(mask, s, -jnp.inf)
    m = jnp.max(s, axis=-1)
    p = jnp.exp(s - m[..., None])
    return m, jnp.sum(p, axis=-1), p


def attn_block(qs, kv, mask):
    f32 = jnp.float32
    q_nope, q_rope, dq = qs
    k_nope, k_rope, v_mla, dk, dv = kv
    s = (jnp.einsum('bqhd,bkhd->bhqk', q_nope, k_nope, preferred_element_type=f32)
         + jnp.einsum('bqhr,bkr->bhqk', q_rope, k_rope, preferred_element_type=f32)) * MLA_SCALE
    m1, l1, p1 = softmax_stats(s, mask)
    a1 = jnp.einsum('bhqk,bkhd->bhqd', p1, v_mla.astype(f32))
    s2 = jnp.einsum('bqhmd,bkhmd->bhmqk', dq, dk, preferred_element_type=f32) * DIFF_SCALE
    m2, l2, p2 = softmax_stats(s2, mask)
    a2 = jnp.einsum('bhmqk,bkhd->bhmqd', p2, dv.astype(f32))
    return ((m1, l1, a1), (m2, l2, a2))


def merge(sa, sb):
    out = []
    for (m1, l1, c1), (m2, l2, c2) in zip(sa, sb):
        m = jnp.maximum(m1, m2)
        e1, e2 = jnp.exp(m1 - m), jnp.exp(m2 - m)
        out.append((m, l1 * e1 + l2 * e2, c1 * e1[..., None] + c2 * e2[..., None]))
    return tuple(out)


def finish(stats, lam, lam_init, g_sub):
    (_, l1, a1), (_, l2, a2) = stats
    b, _, q, _ = a1.shape
    o_mla = (a1 / l1[..., None]).transpose(0, 2, 1, 3).reshape(b, q, MLA_HEADS * MLA_VDIM)
    o_d = a2 / l2[..., None]
    o_d = o_d[:, :, 0] - lam * o_d[:, :, 1]
    o_d = rms_norm(o_d, g_sub) * (1.0 - lam_init)
    o_d = o_d.transpose(0, 2, 1, 3).reshape(b, q, DIFF_HEADS * DIFF_VD)
    return jnp.concatenate([o_mla, o_d], axis=-1)


def attend_prompt(qs, rows, w_uk, w_uv, g_kn, lam, lam_init, g_sub):
    c_kv, k_r, dk, dv = rows
    k_nope, v_mla = mla_expand(c_kv, w_uk, w_uv, g_kn)
    kv = (k_nope, k_r, v_mla, split_dk(dk), dv)
    b, t_pad = c_kv.shape[0], c_kv.shape[1]
    kpos = jnp.arange(t_pad)

    def one_block(i):
        start = i * Q_BLOCK
        qb = tuple(lax.dynamic_slice_in_dim(a, start, Q_BLOCK, axis=1) for a in qs)
        mask = (start + jnp.arange(Q_BLOCK))[:, None] >= kpos[None, :]
        return finish(attn_block(qb, kv, mask), lam, lam_init, g_sub)

    out = lax.map(one_block, jnp.arange(t_pad // Q_BLOCK))
    return jnp.swapaxes(out, 0, 1).reshape(b, t_pad, MIX_WIDTH)


def attend_sample(qs, rows, cache_ckv, cache_kr, cache_dk, cache_dv, page_table, layer,
                  w_uk, w_uv, g_kn, lam, lam_init, g_sub):
    c_kv, k_r, dk, dv = rows
    k_nope, v_mla = mla_expand(c_kv, w_uk, w_uv, g_kn)
    ds = c_kv.shape[1]
    self_mask = jnp.tril(jnp.ones((ds, ds), dtype=bool))
    stats = attn_block(qs, (k_nope, k_r, v_mla, split_dk(dk), dv), self_mask)

    def page_step(st, phys):
        kn_p, v_p = mla_expand(cache_ckv[layer, phys], w_uk, w_uv, g_kn)
        kv_p = (kn_p, cache_kr[layer, phys], v_p, split_dk(cache_dk[layer, phys]), cache_dv[layer, phys])
        return merge(st, attn_block(qs, kv_p, None)), None

    stats, _ = lax.scan(page_step, stats, page_table.T)
    return finish(stats, lam, lam_init, g_sub)


def peer(h, w_pq, sub_keys, expert_u, expert_v):
    shp = h.shape
    hf = h.reshape(-1, D_MODEL)
    n = hf.shape[0]
    n_pad = -(-n // PEER_BLOCK) * PEER_BLOCK
    hf = jnp.pad(hf, ((0, n_pad - n), (0, 0)))
    q = (hf @ w_pq).reshape(n_pad, PEER_HEADS, 2, PEER_QDIM // 2)
    s = jnp.einsum('nhcd,hckd->nhck', q, sub_keys, preferred_element_type=jnp.float32)
    sv, si = lax.top_k(s, PEER_TOPK)
    cand = (sv[:, :, 0, :, None] + sv[:, :, 1, None, :]).reshape(n_pad, PEER_HEADS, PEER_TOPK * PEER_TOPK)
    cand_idx = (si[:, :, 0, :, None] * N_KEYS + si[:, :, 1, None, :]).reshape(n_pad, PEER_HEADS, PEER_TOPK * PEER_TOPK)
    best, sel = lax.top_k(cand, PEER_TOPK)
    idx = jnp.take_along_axis(cand_idx, sel, axis=-1)
    gate = jax.nn.softmax(best, axis=-1)

    def block(args):
        xb, ib, gb = args
        u = expert_u[ib]
        v = expert_v[ib]
        a = jax.nn.gelu(jnp.einsum('nd,nhkd->nhk', xb, u, preferred_element_type=jnp.float32), approximate=False)
        return jnp.einsum('nhk,nhkd->nd', (gb * a).astype(v.dtype), v)

    nb = n_pad // PEER_BLOCK
    out = lax.map(block, (hf.reshape(nb, PEER_BLOCK, D_MODEL),
                          idx.reshape(nb, PEER_BLOCK, PEER_HEADS, PEER_TOPK),
                          gate.reshape(nb, PEER_BLOCK, PEER_HEADS, PEER_TOPK)))
    return out.reshape(n_pad, D_MODEL)[:n].reshape(shp).astype(h.dtype)


def layer_tail(h, o, w_o, g_ffn, w_pq, sub_keys, expert_u, expert_v):
    h = h + o.astype(h.dtype) @ w_o
    return h + peer(rms_norm(h, g_ffn), w_pq, sub_keys, expert_u, expert_v)


def setup_inputs(seed: int = 0) -> dict:
    key = jax.random.key(seed)
    ks = iter(jax.random.split(key, 48))
    f32 = jnp.float32
    nrm = lambda shape, scale: jax.random.normal(next(ks), shape, f32) * scale
    gain = lambda shape: 1.0 + 0.02 * jax.random.normal(next(ks), shape, f32)
    n_pages = PAST_LEN // PAGE_SIZE
    n_used = DEC_BATCH * n_pages
    n_pool = n_used + max(1, n_used // 4)
    page_table = jax.random.permutation(next(ks), n_pool)[:n_used].reshape(DEC_BATCH, n_pages).astype(jnp.int32)
    return {
        "x_prompt": nrm((BATCH, SEQ, D_MODEL), 1.0),
        "x_sample": nrm((DEC_BATCH, DEC_SEQ, D_MODEL), 1.0),
        "cache_mla_ckv": nrm((DEPTH, n_pool, PAGE_SIZE, MLA_KV_RANK), 1.0),
        "cache_mla_krope": nrm((DEPTH, n_pool, PAGE_SIZE, MLA_ROPE), 1.0),
        "cache_diff_k": nrm((DEPTH, n_pool, PAGE_SIZE, DIFF_HEADS, 2 * DIFF_HD), 1.0),
        "cache_diff_v": nrm((DEPTH, n_pool, PAGE_SIZE, DIFF_HEADS, DIFF_VD), 1.0),
        "page_table": page_table,
        "meta_tokens": nrm((N_META, D_MODEL), 1.0),
        "g_attn": gain((DEPTH, D_MODEL)),
        "w_in": nrm((DEPTH, D_MODEL, IN_DIM), D_MODEL ** -0.5),
        "g_cq": gain((DEPTH, MLA_Q_RANK)),
        "w_uq": nrm((DEPTH, MLA_Q_RANK, MLA_HEADS * (MLA_NOPE + MLA_ROPE)), MLA_Q_RANK ** -0.5),
        "g_ckv": gain((DEPTH, MLA_KV_RANK)),
        "w_uk": nrm((DEPTH, MLA_KV_RANK, MLA_HEADS * MLA_NOPE), MLA_KV_RANK ** -0.5),
        "w_uv": nrm((DEPTH, MLA_KV_RANK, MLA_HEADS * MLA_VDIM), MLA_KV_RANK ** -0.5),
        "g_qn": gain((DEPTH, MLA_NOPE)),
        "g_qr": gain((DEPTH, MLA_ROPE)),
        "g_kn": gain((DEPTH, MLA_NOPE)),
        "g_kr": gain((DEPTH, MLA_ROPE)),
        "g_dq": gain((DEPTH, DIFF_HD)),
        "g_dk": gain((DEPTH, DIFF_HD)),
        "lambda_q1": nrm((DEPTH, DIFF_HD), 0.1),
        "lambda_k1": nrm((DEPTH, DIFF_HD), 0.1),
        "lambda_q2": nrm((DEPTH, DIFF_HD), 0.1),
        "lambda_k2": nrm((DEPTH, DIFF_HD), 0.1),
        "g_sub": gain((DEPTH, DIFF_VD)),
        "w_o": nrm((DEPTH, MIX_WIDTH, D_MODEL), MIX_WIDTH ** -0.5),
        "g_ffn": gain((DEPTH, D_MODEL)),
        "w_pq": nrm((DEPTH, D_MODEL, PEER_HEADS * PEER_QDIM), D_MODEL ** -0.5),
        "sub_keys": nrm((DEPTH, PEER_HEADS, 2, N_KEYS, PEER_QDIM // 2), (PEER_QDIM // 2) ** -0.5),
        "expert_u": nrm((DEPTH, N_EXPERTS, D_MODEL), D_MODEL ** -0.5),
        "expert_v": nrm((DEPTH, N_EXPERTS, D_MODEL), 0.5),
    }


def reference(x_prompt, x_sample, cache_mla_ckv, cache_mla_krope, cache_diff_k, cache_diff_v, page_table,
              meta_tokens, g_attn, w_in, g_cq, w_uq, g_ckv, w_uk, w_uv, g_qn, g_qr, g_kn, g_kr, g_dq, g_dk,
              lambda_q1, lambda_k1, lambda_q2, lambda_k2, g_sub, w_o, g_ffn, w_pq, sub_keys, expert_u, expert_v):
    f32 = jnp.float32
    b, s_len, _ = x_prompt.shape
    t = N_META + s_len
    t_pad = -(-t // Q_BLOCK) * Q_BLOCK
    meta = jnp.broadcast_to(meta_tokens[None].astype(x_prompt.dtype), (b, N_META, D_MODEL))
    hp = jnp.pad(jnp.concatenate([meta, x_prompt], axis=1), ((0, 0), (0, t_pad - t), (0, 0)))
    pos_p = jnp.arange(t_pad)
    hs = x_sample
    pos_s = page_table.shape[1] * PAGE_SIZE + jnp.arange(x_sample.shape[1])
    p_rows = ([], [], [], [])
    s_rows = ([], [], [], [])
    for l in range(DEPTH):
        lam_init = 0.8 - 0.6 * math.exp(-0.3 * l)
        lam = (jnp.exp(jnp.sum(lambda_q1[l].astype(f32) * lambda_k1[l].astype(f32)))
               - jnp.exp(jnp.sum(lambda_q2[l].astype(f32) * lambda_k2[l].astype(f32))) + lam_init)
        proj = (w_in[l], g_cq[l], w_uq[l], g_ckv[l], g_qn[l], g_qr[l], g_kr[l], g_dq[l], g_dk[l])
        up = (w_uk[l], w_uv[l], g_kn[l])
        tail = (w_o[l], g_ffn[l], w_pq[l], sub_keys[l], expert_u[l], expert_v[l])
        qs, rows = mixer_inputs(rms_norm(hp, g_attn[l]), pos_p, *proj)
        o = attend_prompt(qs, rows, *up, lam, lam_init, g_sub[l])
        hp = layer_tail(hp, o, *tail)
        for acc, r in zip(p_rows, rows):
            acc.append(r[:, :t])
        qs, rows = mixer_inputs(rms_norm(hs, g_attn[l]), pos_s, *proj)
        o = attend_sample(qs, rows, cache_mla_ckv, cache_mla_krope, cache_diff_k, cache_diff_v, page_table, l,
                          *up, lam, lam_init, g_sub[l])
        hs = layer_tail(hs, o, *tail)
        for acc, r in zip(s_rows, rows):
            acc.append(r)
    y_prompt = hp[:, N_META:t]
    y_sample = hs
    return (y_prompt, y_sample,
            jnp.stack(p_rows[0]), jnp.stack(p_rows[1]), jnp.stack(p_rows[2]), jnp.stack(p_rows[3]),
            jnp.stack(s_rows[0]), jnp.stack(s_rows[1]), jnp.stack(s_rows[2]), jnp.stack(s_rows[3]))
```

```python
import functools
import math

import numpy as np
import jax
import jax.numpy as jnp
from jax import lax
from jax.experimental import pallas as pl
from jax.experimental.pallas import tpu as pltpu

F32 = jnp.float32
BF16 = jnp.bfloat16

D_MODEL = 1024
N_META = 16
ROPE_THETA = 10000.0
EPS = 1e-6
MLA_HEADS = 8
MLA_NOPE = 64
MLA_ROPE = 32
MLA_VDIM = 64
MLA_Q_RANK = 256
MLA_KV_RANK = 128
MLA_SCALE = (MLA_NOPE + MLA_ROPE) ** -0.5
DIFF_HEADS = 4
DIFF_HD = 64
DIFF_VD = 128
DIFF_SCALE = DIFF_HD ** -0.5
PEER_HEADS = 8
N_KEYS = 128
PEER_TOPK = 16
PAGE = 128
LANES = 128
N_ATTN = MLA_HEADS + 2 * DIFF_HEADS
NEG = -1e30
VMEM_LIMIT = 56 * 1024 * 1024

_NT = (((1,), (1,)), ((), ()))


def _round_up(n, m):
    return -(-n // m) * m


def _tile(n, pref):
    t = min(pref, n)
    assert n % t == 0, (n, t)
    return t


def _params(sem):
    return pltpu.CompilerParams(dimension_semantics=sem, vmem_limit_bytes=VMEM_LIMIT)


def _full(shape):
    nd = len(shape)
    return pl.BlockSpec(shape, lambda *_: (0,) * nd)


def _group_mean(xx, s):
    hi = xx.astype(BF16)
    lo = (xx - hi.astype(F32)).astype(BF16)
    return jnp.dot(hi, s, preferred_element_type=F32) + jnp.dot(lo, s, preferred_element_type=F32)


def _proj_kernel(x_ref, cm_ref, sm_ref, cd_ref, sd_ref, gattn_ref, win_ref, gcq_ref, wuq_ref, gckv_ref,
                 wuk_ref, smla_ref, sdiff_ref, gq_ref, gk_ref, gkr_ref, gdq_ref, gdk_ref,
                 q_ref, km_ref, ckv_ref, ckvb_ref, kr_ref, dk_ref, dkb_ref, dv_ref, dvb_ref):
    tm = x_ref.shape[0]
    x = x_ref[...]
    hn = x * lax.rsqrt(jnp.mean(x * x, axis=-1, keepdims=True) + EPS) * gattn_ref[...]
    z = jnp.dot(hn.astype(BF16), win_ref[...], preferred_element_type=F32)
    lane = lax.broadcasted_iota(jnp.int32, (tm, LANES), 1)
    cm, sm, cd, sd = cm_ref[...], sm_ref[...], cd_ref[...], sd_ref[...]
    smla, sdiff = smla_ref[...], sdiff_ref[...]
    mla_lo = (lane >= MLA_NOPE) & (lane < MLA_NOPE + MLA_ROPE // 2)
    diff_lo = (lane % DIFF_HD) < DIFF_HD // 2
    first_half = lane < DIFF_HD

    def rope_mla(v):
        sw = jnp.where(mla_lo, pltpu.roll(v, LANES - MLA_ROPE // 2, 1), pltpu.roll(v, MLA_ROPE // 2, 1))
        return v * cm + sw * sm

    def rope_diff(v):
        sw = jnp.where(diff_lo, pltpu.roll(v, LANES - DIFF_HD // 2, 1), pltpu.roll(v, DIFF_HD // 2, 1))
        return v * cd + sw * sd

    def group_norm(v, s, g):
        return v * lax.rsqrt(_group_mean(v * v, s) + EPS) * g

    cq = z[:, :MLA_Q_RANK]
    cqn = cq * lax.rsqrt(jnp.mean(cq * cq, axis=-1, keepdims=True) + EPS) * gcq_ref[...]
    q = jnp.dot(cqn.astype(BF16), wuq_ref[...], preferred_element_type=F32)
    for h in range(MLA_HEADS):
        blk = group_norm(q[:, h * LANES:(h + 1) * LANES], smla, gq_ref[...])
        q_ref[:, h * LANES:(h + 1) * LANES] = (rope_mla(blk) * MLA_SCALE).astype(BF16)

    o = MLA_Q_RANK
    ckv = z[:, o:o + MLA_KV_RANK]
    ckvn = ckv * lax.rsqrt(jnp.mean(ckv * ckv, axis=-1, keepdims=True) + EPS) * gckv_ref[...]
    ckv_ref[...] = ckvn
    ckvn_b = ckvn.astype(BF16)
    ckvb_ref[...] = ckvn_b

    o_kr = MLA_Q_RANK + MLA_KV_RANK + 3 * DIFF_HEADS * DIFF_VD
    kr = rope_mla(group_norm(z[:, o_kr:o_kr + LANES], smla, gkr_ref[...]))
    kr_ref[...] = kr

    kpre = jnp.dot(ckvn_b, wuk_ref[...], preferred_element_type=F32)
    for h in range(MLA_HEADS):
        kb = group_norm(kpre[:, h * LANES:(h + 1) * LANES], smla, gk_ref[...])
        km_ref[:, h * LANES:(h + 1) * LANES] = (kb + kr).astype(BF16)

    o_dq = MLA_Q_RANK + MLA_KV_RANK
    o_dk = o_dq + DIFF_HEADS * DIFF_VD
    o_dv = o_dk + DIFF_HEADS * DIFF_VD
    base = MLA_HEADS * LANES
    for h in range(DIFF_HEADS):
        blk = group_norm(z[:, o_dq + h * LANES:o_dq + (h + 1) * LANES], sdiff, gdq_ref[...])
        blk = rope_diff(blk) * DIFF_SCALE
        q_ref[:, base + 2 * h * LANES:base + (2 * h + 1) * LANES] = jnp.where(first_half, blk, 0.0).astype(BF16)
        q_ref[:, base + (2 * h + 1) * LANES:base + (2 * h + 2) * LANES] = jnp.where(first_half, 0.0, blk).astype(BF16)
        kb = rope_diff(group_norm(z[:, o_dk + h * LANES:o_dk + (h + 1) * LANES], sdiff, gdk_ref[...]))
        dk_ref[:, h * LANES:(h + 1) * LANES] = kb
        dkb_ref[:, h * LANES:(h + 1) * LANES] = kb.astype(BF16)
    dv = z[:, o_dv:o_dv + DIFF_HEADS * DIFF_VD]
    dv_ref[...] = dv
    dvb_ref[...] = dv.astype(BF16)


def _proj(rows, tabs, consts, tm):
    r = rows.shape[0]
    row = lambda w: pl.BlockSpec((tm, w), lambda i: (i, 0))
    in_specs = [row(D_MODEL)] + [row(LANES)] * 4 + [_full(c.shape) for c in consts]
    widths = (N_ATTN * LANES, MLA_HEADS * LANES, LANES, LANES, LANES, 512, 512, 512, 512)
    dts = (BF16, BF16, F32, BF16, F32, F32, BF16, F32, BF16)
    return pl.pallas_call(
        _proj_kernel,
        grid=(r // tm,),
        in_specs=in_specs,
        out_specs=[row(w) for w in widths],
        out_shape=[jax.ShapeDtypeStruct((r, w), d) for w, d in zip(widths, dts)],
        compiler_params=_params(("parallel",)),
        name="proj",
    )(rows, *tabs, *consts)


def _lam_value(lam_ref, lam_init):
    lv = lam_ref[...]
    a = jnp.sum(lv[0:1] * lv[1:2], axis=-1, keepdims=True)
    b = jnp.sum(lv[2:3] * lv[3:4], axis=-1, keepdims=True)
    return jnp.exp(a) - jnp.exp(b) + lam_init


def _pattn_kernel(qi_ref, ki_ref, q_ref, km_ref, ckvb_ref, dkb_ref, dvb_ref, wuv_ref, lam_ref, gsub_ref,
                  o_ref, m_scr, l_scr, acc_scr, *, lam_init):
    s = pl.program_id(1)
    qi, ki = qi_ref[s], ki_ref[s]
    tq, tk = q_ref.shape[0], km_ref.shape[0]

    @pl.when(ki == 0)
    def _():
        m_scr[...] = jnp.full(m_scr.shape, NEG, F32)
        l_scr[...] = jnp.zeros(l_scr.shape, F32)
        acc_scr[...] = jnp.zeros(acc_scr.shape, F32)

    def step(masked):
        if masked:
            keep = (lax.broadcasted_iota(jnp.int32, (tq, tk), 0) >= lax.broadcasted_iota(jnp.int32, (tq, tk), 1))
        for hh in range(N_ATTN):
            q = q_ref[:, hh * LANES:(hh + 1) * LANES]
            if hh < MLA_HEADS:
                k = km_ref[:, hh * LANES:(hh + 1) * LANES]
                v = ckvb_ref[...]
            else:
                hd = (hh - MLA_HEADS) // 2
                k = dkb_ref[:, hd * LANES:(hd + 1) * LANES]
                v = dvb_ref[:, hd * LANES:(hd + 1) * LANES]
            sc = lax.dot_general(q, k, _NT, preferred_element_type=F32)
            if masked:
                sc = jnp.where(keep, sc, NEG)
            m_prev = m_scr[hh]
            m_new = jnp.maximum(m_prev, jnp.max(sc, axis=1, keepdims=True))
            alpha = jnp.exp(m_prev - m_new)
            p = jnp.exp(sc - jnp.concatenate([m_new] * (tk // LANES), axis=1))
            l_scr[hh] = alpha * l_scr[hh] + jnp.sum(p, axis=1, keepdims=True)
            acc_scr[hh] = alpha * acc_scr[hh] + jnp.dot(p.astype(BF16), v, preferred_element_type=F32)
            m_scr[hh] = m_new

    @pl.when(ki < qi)
    def _():
        step(False)

    @pl.when(ki == qi)
    def _():
        step(True)
        o_mla = jnp.zeros((tq, MLA_HEADS * MLA_VDIM), F32)
        for h in range(MLA_HEADS):
            a = (acc_scr[h] / l_scr[h]).astype(BF16)
            o_mla = o_mla + jnp.dot(a, wuv_ref[h], preferred_element_type=F32)
        o_ref[:, :MLA_HEADS * MLA_VDIM] = o_mla.astype(o_ref.dtype)
        lam = _lam_value(lam_ref, lam_init)
        for hd in range(DIFF_HEADS):
            h0 = MLA_HEADS + 2 * hd
            od = acc_scr[h0] / l_scr[h0] - lam * (acc_scr[h0 + 1] / l_scr[h0 + 1])
            od = od * lax.rsqrt(jnp.mean(od * od, axis=-1, keepdims=True) + EPS) * gsub_ref[...] * (1.0 - lam_init)
            c0 = MLA_HEADS * MLA_VDIM + hd * DIFF_VD
            o_ref[:, c0:c0 + DIFF_VD] = od.astype(o_ref.dtype)


def _pattn(q_all, km, ckvb, dkb, dvb, wuv_p, lam_vecs, gsub, *, batch, t_pad, tq, lam_init):
    nq = t_pad // tq
    pairs = [(i, j) for i in range(nq) for j in range(i + 1)]
    qi_tab = jnp.asarray([p[0] for p in pairs], jnp.int32)
    ki_tab = jnp.asarray([p[1] for p in pairs], jnp.int32)
    qmap = lambda b, s, qi, ki: (b * nq + qi[s], 0)
    kmap = lambda b, s, qi, ki: (b * nq + ki[s], 0)
    cmap = lambda nd: (lambda b, s, qi, ki: (0,) * nd)
    grid_spec = pltpu.PrefetchScalarGridSpec(
        num_scalar_prefetch=2,
        grid=(batch, len(pairs)),
        in_specs=[
            pl.BlockSpec((tq, N_ATTN * LANES), qmap),
            pl.BlockSpec((tq, MLA_HEADS * LANES), kmap),
            pl.BlockSpec((tq, LANES), kmap),
            pl.BlockSpec((tq, DIFF_HEADS * LANES), kmap),
            pl.BlockSpec((tq, DIFF_HEADS * LANES), kmap),
            pl.BlockSpec(wuv_p.shape, cmap(3)),
            pl.BlockSpec(lam_vecs.shape, cmap(2)),
            pl.BlockSpec(gsub.shape, cmap(2)),
        ],
        out_specs=pl.BlockSpec((tq, D_MODEL), qmap),
        scratch_shapes=[pltpu.VMEM((N_ATTN, tq, LANES), F32)] * 3,
    )
    return pl.pallas_call(
        functools.partial(_pattn_kernel, lam_init=lam_init),
        grid_spec=grid_spec,
        out_shape=jax.ShapeDtypeStruct((batch * t_pad, D_MODEL), BF16),
        compiler_params=_params(("parallel", "arbitrary")),
        name="pattn",
    )(qi_tab, ki_tab, q_all, km, ckvb, dkb, dvb, wuv_p, lam_vecs, gsub)


def _qabs_kernel(q_ref, w_ref, o_ref):
    for h in range(MLA_HEADS):
        o_ref[:, h * LANES:(h + 1) * LANES] = jnp.dot(
            q_ref[:, h * LANES:(h + 1) * LANES], w_ref[h], preferred_element_type=F32).astype(o_ref.dtype)


def _qabs(q_mla, wabs):
    return pl.pallas_call(
        _qabs_kernel,
        out_shape=jax.ShapeDtypeStruct(q_mla.shape, BF16),
        name="qabs",
    )(q_mla, wabs)


def _sattn_kernel(pt_ref, qabs_ref, qrope_ref, qd_ref, sckv_ref, skr_ref, sdk_ref, sdv_ref, wukt_ref, wuv_ref,
                  lam_ref, gsub_ref, *rest, n_group, dec_seq, lam_init):
    pages = rest[:4 * n_group]
    o_ref, m_scr, l_scr, accm_scr, accd_scr = rest[4 * n_group:]
    s = pl.program_id(1)
    nq = dec_seq * MLA_HEADS
    qabs, qrope, qd = qabs_ref[0], qrope_ref[0], qd_ref[0]

    def scores(ckv_b, kr_b, dk_b):
        kt = lax.dot_general(wukt_ref[...], ckv_b, _NT, preferred_element_type=F32)
        ms = jnp.sum((kt * kt).reshape(MLA_HEADS, MLA_NOPE, kt.shape[1]), axis=1) * (1.0 / MLA_NOPE)
        r = lax.rsqrt(ms + EPS)
        sn = lax.dot_general(qabs, ckv_b, _NT, preferred_element_type=F32)
        sr = lax.dot_general(qrope, kr_b, _NT, preferred_element_type=F32)
        sd = lax.dot_general(qd, dk_b, _NT, preferred_element_type=F32)
        return jnp.concatenate([sn * jnp.concatenate([r] * dec_seq, axis=0) + sr, sd], axis=0)

    def update(pieces):
        m_prev = m_scr[...]
        m_new = m_prev
        for sc, _, _ in pieces:
            m_new = jnp.maximum(m_new, jnp.max(sc, axis=1, keepdims=True))
        alpha = jnp.exp(m_prev - m_new)
        l_new = alpha * l_scr[...]
        accm = alpha[:nq] * accm_scr[...]
        accd = alpha[nq:] * accd_scr[...]
        for sc, ckv_b, dv_b in pieces:
            p = jnp.exp(sc - m_new)
            l_new = l_new + jnp.sum(p, axis=1, keepdims=True)
            pb = p.astype(BF16)
            accm = accm + jnp.dot(pb[:nq], ckv_b, preferred_element_type=F32)
            accd = accd + jnp.dot(pb[nq:], dv_b, preferred_element_type=F32)
        m_scr[...] = m_new
        l_scr[...] = l_new
        accm_scr[...] = accm
        accd_scr[...] = accd

    @pl.when(s == 0)
    def _():
        m_scr[...] = jnp.full(m_scr.shape, NEG, F32)
        l_scr[...] = jnp.zeros(l_scr.shape, F32)
        accm_scr[...] = jnp.zeros(accm_scr.shape, F32)
        accd_scr[...] = jnp.zeros(accd_scr.shape, F32)
        sc = scores(sckv_ref[0], skr_ref[0], sdk_ref[0])
        rq = (lax.broadcasted_iota(jnp.int32, sc.shape, 0) % nq) // MLA_HEADS
        col = lax.broadcasted_iota(jnp.int32, sc.shape, 1)
        update([(jnp.where(col <= rq, sc, NEG), sckv_ref[0], sdv_ref[0])])

    pieces = []
    for g in range(n_group):
        ckv_b = pages[g][...].astype(BF16)
        kr_b = pages[n_group + g][...].astype(BF16)
        dk_b = pages[2 * n_group + g][...].astype(BF16)
        dv_b = pages[3 * n_group + g][...].astype(BF16)
        pieces.append((scores(ckv_b, kr_b, dk_b), ckv_b, dv_b))
    update(pieces)

    @pl.when(s == pl.num_programs(1) - 1)
    def _():
        l = l_scr[...]
        row = lax.broadcasted_iota(jnp.int32, (nq, 512), 0) % MLA_HEADS
        lane = lax.broadcasted_iota(jnp.int32, (nq, 512), 1)
        am = (accm_scr[...] / l[:nq]).astype(BF16)
        y = jnp.dot(am, wuv_ref[...], preferred_element_type=F32)
        y = jnp.where(lane // MLA_VDIM == row, y, 0.0)
        o_mla = jnp.sum(y.reshape(dec_seq, MLA_HEADS, 512), axis=1)
        lam = _lam_value(lam_ref, lam_init)
        ad = accd_scr[...] / l[nq:]
        coef = jnp.where(row % 2 == 0, 1.0, -lam)
        ad = jnp.where(lane // DIFF_VD == row // 2, ad * coef, 0.0)
        od = jnp.sum(ad.reshape(dec_seq, MLA_HEADS, 512), axis=1)
        o_ref[0, :, :512] = o_mla
        for hd in range(DIFF_HEADS):
            blk = od[:, hd * DIFF_VD:(hd + 1) * DIFF_VD]
            blk = blk * lax.rsqrt(jnp.mean(blk * blk, axis=-1, keepdims=True) + EPS) * gsub_ref[...] * (1.0 - lam_init)
            o_ref[0, :, 512 + hd * DIFF_VD:512 + (hd + 1) * DIFF_VD] = blk


def _sattn(page_table, qabs, qrope, qd, self_ckv, self_kr, self_dk, self_dv, wukt, wuv, lam_vecs, gsub,
           cache_ckv, cache_kr, cache_dk, cache_dv, *, layer, n_group, lam_init):
    db, n_pages = page_table.shape
    dec_seq = qabs.shape[1] // MLA_HEADS
    nq = dec_seq * MLA_HEADS
    assert n_pages % n_group == 0
    pt_flat = page_table.reshape(-1)
    seq = lambda shape: pl.BlockSpec((1,) + shape, lambda b, s, pt: (b, 0, 0))
    const = lambda shape: pl.BlockSpec(shape, lambda b, s, pt: (0,) * len(shape))

    def page_spec(width, g):
        return pl.BlockSpec((None, None, PAGE, width),
                            lambda b, s, pt: (layer, pt[b * n_pages + s * n_group + g], 0, 0))

    in_specs = [seq((nq, LANES)), seq((nq, MLA_ROPE)), seq((nq, 512)),
                seq((PAGE, LANES)), seq((PAGE, MLA_ROPE)), seq((PAGE, 512)), seq((PAGE, 512)),
                const(wukt.shape), const(wuv.shape), const(lam_vecs.shape), const(gsub.shape)]
    operands = [qabs, qrope, qd, self_ckv, self_kr, self_dk, self_dv, wukt, wuv, lam_vecs, gsub]
    for arr, width in ((cache_ckv, LANES), (cache_kr, MLA_ROPE), (cache_dk, 512), (cache_dv, 512)):
        for g in range(n_group):
            in_specs.append(page_spec(width, g))
            operands.append(arr)
    grid_spec = pltpu.PrefetchScalarGridSpec(
        num_scalar_prefetch=1,
        grid=(db, n_pages // n_group),
        in_specs=in_specs,
        out_specs=pl.BlockSpec((1, dec_seq, D_MODEL), lambda b, s, pt: (b, 0, 0)),
        scratch_shapes=[pltpu.VMEM((2 * nq, 1), F32), pltpu.VMEM((2 * nq, 1), F32),
                        pltpu.VMEM((nq, LANES), F32), pltpu.VMEM((nq, 512), F32)],
    )
    return pl.pallas_call(
        functools.partial(_sattn_kernel, n_group=n_group, dec_seq=dec_seq, lam_init=lam_init),
        grid_spec=grid_spec,
        out_shape=jax.ShapeDtypeStruct((db, dec_seq, D_MODEL), F32),
        compiler_params=_params(("parallel", "arbitrary")),
        name="sattn",
    )(pt_flat, *operands)


def _tail_kernel(x_ref, o_ref, wo_ref, gffn_ref, wpq_ref, kbdt_ref, h1_ref, hn_ref, st_ref):
    h1 = x_ref[...] + jnp.dot(o_ref[...].astype(BF16), wo_ref[...], preferred_element_type=F32)
    h1_ref[...] = h1
    hn = (h1 * lax.rsqrt(jnp.mean(h1 * h1, axis=-1, keepdims=True) + EPS) * gffn_ref[...]).astype(BF16)
    hn_ref[...] = hn
    q = jnp.dot(hn, wpq_ref[...], preferred_element_type=F32).astype(BF16)
    st_ref[...] = lax.dot_general(kbdt_ref[...], q, _NT, preferred_element_type=F32)


def _tail(x, o, wo, gffn, wpq, kbdt, tm):
    n = x.shape[0]
    row = lambda w: pl.BlockSpec((tm, w), lambda i: (i, 0))
    n_sc = kbdt.shape[0]
    return pl.pallas_call(
        _tail_kernel,
        grid=(n // tm,),
        in_specs=[row(D_MODEL), row(D_MODEL), _full(wo.shape), _full(gffn.shape), _full(wpq.shape), _full(kbdt.shape)],
        out_specs=[row(D_MODEL), row(D_MODEL), pl.BlockSpec((n_sc, tm), lambda i: (0, i))],
        out_shape=[jax.ShapeDtypeStruct((n, D_MODEL), F32), jax.ShapeDtypeStruct((n, D_MODEL), BF16),
                   jax.ShapeDtypeStruct((n_sc, n), F32)],
        compiler_params=_params(("parallel",)),
        name="tail",
    )(x, o, wo, gffn, wpq, kbdt)


_CAND_ROWS = [(a, (PEER_TOPK + 1) // (a + 1)) for a in range(PEER_TOPK + 1)]


def _extract_max(x, ridx, n_rows):
    m = jnp.max(x, axis=0, keepdims=True)
    first = jnp.min(jnp.where(x == m, ridx, n_rows), axis=0, keepdims=True)
    return m, jnp.where(ridx == first, -jnp.inf, x)


def _thr_kernel(st_ref, tp_ref, ci_ref, ej_ref):
    tn = st_ref.shape[1]
    ridx = lax.broadcasted_iota(jnp.int32, (N_KEYS, tn), 0)
    n_cand = sum(nb for _, nb in _CAND_ROWS)
    cidx = lax.broadcasted_iota(jnp.int32, (n_cand, tn), 0)
    for h in range(PEER_HEADS):
        si = st_ref[h * 2 * N_KEYS:h * 2 * N_KEYS + N_KEYS, :]
        sj = st_ref[h * 2 * N_KEYS + N_KEYS:(h + 1) * 2 * N_KEYS, :]
        tops = []
        for sc in (si, sj):
            x, vals = sc, []
            for _ in range(PEER_TOPK + 1):
                m, x = _extract_max(x, ridx, N_KEYS)
                vals.append(m)
            tops.append(jnp.concatenate(vals, axis=0))
        ti, tj = tops
        mi, mj = ti[0:1], tj[0:1]
        eti, etj = jnp.exp(ti - mi), jnp.exp(tj - mj)
        cand = jnp.concatenate([ti[a:a + 1] + tj[0:nb] for a, nb in _CAND_ROWS], axis=0)
        ecand = jnp.concatenate([eti[a:a + 1] * etj[0:nb] for a, nb in _CAND_ROWS], axis=0)
        x = cand
        for it in range(PEER_TOPK + 1):
            m, x = _extract_max(x, cidx, n_cand)
            if it == PEER_TOPK - 1:
                v16 = m
        thr = 0.5 * (v16 + m)
        z = jnp.sum(jnp.where(cand >= thr, ecand, 0.0), axis=0, keepdims=True)
        tp_ref[h * N_KEYS:(h + 1) * N_KEYS, :] = thr - si
        ci_ref[h * N_KEYS:(h + 1) * N_KEYS, :] = jnp.exp(si - mi) / z
        ej_ref[h * N_KEYS:(h + 1) * N_KEYS, :] = jnp.exp(sj - mj)


def _thr(st, tn):
    n = st.shape[1]
    rows = PEER_HEADS * N_KEYS
    out = pl.BlockSpec((rows, tn), lambda i: (0, i))
    return pl.pallas_call(
        _thr_kernel,
        grid=(n // tn,),
        in_specs=[pl.BlockSpec((st.shape[0], tn), lambda i: (0, i))],
        out_specs=[out] * 3,
        out_shape=[jax.ShapeDtypeStruct((rows, n), F32)] * 3,
        compiler_params=_params(("parallel",)),
        name="thr",
    )(st)


def _peer_kernel(hnt_ref, u_ref, vt_ref, tp_ref, ci_ref, sj_ref, ej_ref, h1t_ref, y_ref, *, n_i):
    e = pl.program_id(1)
    tn = hnt_ref.shape[1]

    @pl.when(e == 0)
    def _():
        y_ref[...] = h1t_ref[...]

    a = jnp.dot(u_ref[...], hnt_ref[...], preferred_element_type=F32)
    act = 0.5 * a * (1.0 + lax.erf(a * (2.0 ** -0.5)))
    rows = [[(tp_ref[h, pl.ds(e * n_i + ii, 1), :], ci_ref[h, pl.ds(e * n_i + ii, 1), :])
             for h in range(PEER_HEADS)] for ii in range(n_i)]
    cols = []
    for lc in range(tn // LANES):
        ls = slice(lc * LANES, (lc + 1) * LANES)
        parts = []
        for ii in range(n_i):
            g = None
            for h in range(PEER_HEADS):
                tt, cc = rows[ii][h][0][:, ls], rows[ii][h][1][:, ls]
                term = jnp.where(sj_ref[h, 0, :, ls] >= tt, ej_ref[h, :, ls], 0.0) * cc
                g = term if g is None else g + term
            parts.append((act[ii * N_KEYS:(ii + 1) * N_KEYS, ls] * g).astype(BF16))
        cols.append(jnp.concatenate(parts, axis=0))
    masked = jnp.concatenate(cols, axis=1)
    y_ref[...] += jnp.dot(vt_ref[...], masked, preferred_element_type=F32)


def _peer(hnt, u_b, vt_b, tp, ci, st4, ej, h1t, *, tn, n_i):
    n = hnt.shape[1]
    te = n_i * N_KEYS
    n_exp = u_b.shape[0]
    tok = lambda rows: pl.BlockSpec((rows, tn), lambda t, e: (0, t))
    hk = pl.BlockSpec((PEER_HEADS, N_KEYS, tn), lambda t, e: (0, 0, t))
    sj_spec = pl.BlockSpec((PEER_HEADS, 1, N_KEYS, tn), lambda t, e: (0, 1, 0, t))
    return pl.pallas_call(
        functools.partial(_peer_kernel, n_i=n_i),
        grid=(n // tn, n_exp // te),
        in_specs=[tok(D_MODEL), pl.BlockSpec((te, D_MODEL), lambda t, e: (e, 0)),
                  pl.BlockSpec((D_MODEL, te), lambda t, e: (0, e)), hk, hk, sj_spec, hk, tok(D_MODEL)],
        out_specs=tok(D_MODEL),
        out_shape=jax.ShapeDtypeStruct((D_MODEL, n), F32),
        compiler_params=_params(("parallel", "arbitrary")),
        name="peer",
    )(hnt, u_b, vt_b, tp, ci, st4, ej, h1t)


def _rope_tables(pos):
    posf = pos.astype(F32)[:, None]
    h16 = MLA_ROPE // 2
    inv16 = ROPE_THETA ** (-jnp.arange(h16, dtype=F32) / h16)
    a16 = posf * inv16[None, :]
    c16, s16 = jnp.cos(a16), jnp.sin(a16)
    n = pos.shape[0]
    ones, zeros = jnp.ones((n, MLA_NOPE), F32), jnp.zeros((n, MLA_NOPE), F32)
    pad1, pad0 = jnp.ones((n, LANES - MLA_NOPE - MLA_ROPE), F32), jnp.zeros((n, LANES - MLA_NOPE - MLA_ROPE), F32)
    cm = jnp.concatenate([ones, c16, c16, pad1], axis=1)
    sm = jnp.concatenate([zeros, -s16, s16, pad0], axis=1)
    h32 = DIFF_HD // 2
    inv32 = ROPE_THETA ** (-jnp.arange(h32, dtype=F32) / h32)
    a32 = posf * inv32[None, :]
    c32, s32 = jnp.cos(a32), jnp.sin(a32)
    cd = jnp.concatenate([c32, c32, c32, c32], axis=1)
    sd = jnp.concatenate([-s32, s32, -s32, s32], axis=1)
    return cm, sm, cd, sd


def _head_pad(w, width):
    k = w.shape[0]
    w = w.reshape(k, -1, width)
    return jnp.pad(w, ((0, 0), (0, 0), (0, LANES - width))).reshape(k, -1)


def _layer_consts(l, g_attn, w_in, g_cq, w_uq, g_ckv, w_uk, g_qn, g_qr, g_kn, g_kr, g_dq, g_dk):
    cuts = np.cumsum((MLA_Q_RANK, MLA_KV_RANK, MLA_ROPE, 512, 512, 512))
    wi = w_in[l]
    w_cq, w_ckv, w_kr = wi[:, :cuts[0]], wi[:, cuts[0]:cuts[1]], wi[:, cuts[1]:cuts[2]]
    w_dq, w_dk, w_dv = wi[:, cuts[2]:cuts[3]], wi[:, cuts[3]:cuts[4]], wi[:, cuts[4]:cuts[5]]
    w_kr = jnp.pad(w_kr, ((0, 0), (MLA_NOPE, LANES - MLA_NOPE - MLA_ROPE)))
    win_p = jnp.concatenate([w_cq, w_ckv, w_dq, w_dk, w_dv, w_kr], axis=1).astype(BF16)
    wuq_p = _head_pad(w_uq[l], MLA_NOPE + MLA_ROPE).astype(BF16)
    wuk_p = _head_pad(w_uk[l], MLA_NOPE).astype(BF16)
    smla = np.zeros((LANES, LANES), np.float32)
    smla[:MLA_NOPE, :MLA_NOPE] = 1.0 / MLA_NOPE
    smla[MLA_NOPE:MLA_NOPE + MLA_ROPE, MLA_NOPE:MLA_NOPE + MLA_ROPE] = 1.0 / MLA_ROPE
    sdiff = np.zeros((LANES, LANES), np.float32)
    sdiff[:DIFF_HD, :DIFF_HD] = 1.0 / DIFF_HD
    sdiff[DIFF_HD:, DIFF_HD:] = 1.0 / DIFF_HD
    z = lambda n: jnp.zeros((n,), F32)
    row = lambda v: v.astype(F32)[None, :]
    gq = row(jnp.concatenate([g_qn[l], g_qr[l], z(LANES - MLA_NOPE - MLA_ROPE)]))
    gk = row(jnp.concatenate([g_kn[l], z(LANES - MLA_NOPE)]))
    gkr = row(jnp.concatenate([z(MLA_NOPE), g_kr[l], z(LANES - MLA_NOPE - MLA_ROPE)]))
    gdq = row(jnp.concatenate([g_dq[l], g_dq[l]]))
    gdk = row(jnp.concatenate([g_dk[l], g_dk[l]]))
    return (row(g_attn[l]), win_p, row(g_cq[l]), wuq_p, row(g_ckv[l]), wuk_p,
            jnp.asarray(smla, BF16), jnp.asarray(sdiff, BF16), gq, gk, gkr, gdq, gdk)


def kernel(x_prompt, x_sample, cache_mla_ckv, cache_mla_krope, cache_diff_k, cache_diff_v, page_table, meta_tokens, g_attn, w_in, g_cq, w_uq, g_ckv, w_uk, w_uv, g_qn, g_qr, g_kn, g_kr, g_dq, g_dk, lambda_q1, lambda_k1, lambda_q2, lambda_k2, g_sub, w_o, g_ffn, w_pq, sub_keys, expert_u, expert_v):
    depth = w_in.shape[0]
    assert depth == 1, "one decoder layer is supported"
    l = 0
    lam_init = 0.8 - 0.6 * math.exp(-0.3 * l)
    b, s_len, _ = x_prompt.shape
    db, ds, _ = x_sample.shape
    n_pages = page_table.shape[1]
    t = N_META + s_len
    tq = 256
    t_pad = _round_up(t, tq)
    tm = 256

    meta = jnp.broadcast_to(meta_tokens[None].astype(x_prompt.dtype), (b, N_META, D_MODEL))
    hp = jnp.pad(jnp.concatenate([meta, x_prompt], axis=1), ((0, 0), (0, t_pad - t), (0, 0)))
    n_p, n_s = b * t_pad, db * ds
    r_pad = _round_up(n_p + n_s, tm)
    rows = jnp.concatenate([hp.reshape(n_p, D_MODEL), x_sample.reshape(n_s, D_MODEL),
                            jnp.zeros((r_pad - n_p - n_s, D_MODEL), F32)], axis=0)
    pos = jnp.concatenate([jnp.tile(jnp.arange(t_pad), b), jnp.tile(n_pages * PAGE + jnp.arange(ds), db),
                           jnp.zeros((r_pad - n_p - n_s,), jnp.int32)])
    consts = _layer_consts(l, g_attn, w_in, g_cq, w_uq, g_ckv, w_uk, g_qn, g_qr, g_kn, g_kr, g_dq, g_dk)
    q_all, km, ckv, ckvb, kr, dk, dkb, dv, dvb = _proj(rows, _rope_tables(pos), consts, tm)

    lam_vecs = jnp.stack([lambda_q1[l], lambda_k1[l], lambda_q2[l], lambda_k2[l]]).astype(F32)
    gsub = g_sub[l].astype(F32)[None, :]
    wuv = w_uv[l].astype(BF16)
    head_of_col = np.arange(MLA_HEADS * MLA_VDIM) // MLA_VDIM
    wuv_p = jnp.where(jnp.asarray(head_of_col[None, None, :] == np.arange(MLA_HEADS)[:, None, None]),
                      wuv[None], jnp.zeros((), BF16))

    o_p = _pattn(q_all, km, ckvb, dkb, dvb, wuv_p, lam_vecs, gsub, batch=b, t_pad=t_pad, tq=tq, lam_init=lam_init)

    sl = slice(n_p, n_p + n_s)
    wabs = (w_uk[l].reshape(MLA_KV_RANK, MLA_HEADS, MLA_NOPE) * g_kn[l][None, None, :]).transpose(1, 2, 0)
    wabs = jnp.pad(wabs, ((0, 0), (0, LANES - MLA_NOPE), (0, 0))).astype(BF16)
    q_s = q_all[sl]
    qabs = _qabs(q_s[:, :MLA_HEADS * LANES], wabs).reshape(db, ds * MLA_HEADS, LANES)
    qrope = q_s[:, :MLA_HEADS * LANES].reshape(db, ds * MLA_HEADS, LANES)[:, :, MLA_NOPE:MLA_NOPE + MLA_ROPE]
    qd_blk = q_s[:, MLA_HEADS * LANES:].reshape(db, ds, DIFF_HEADS, 2, LANES)
    head_eye = jnp.asarray(np.eye(DIFF_HEADS), BF16)
    qd = (qd_blk[:, :, :, :, None, :] * head_eye[None, None, :, None, :, None]).reshape(db, ds * MLA_HEADS, 512)
    pad_tok = lambda a: jnp.pad(a.reshape(db, ds, -1), ((0, 0), (0, PAGE - ds), (0, 0)))
    kr_s = kr[sl][:, MLA_NOPE:MLA_NOPE + MLA_ROPE]
    n_pool = cache_diff_k.shape[1]
    n_group = 8 if n_pages % 8 == 0 else 1
    o_s = _sattn(page_table, qabs, qrope, qd, pad_tok(ckvb[sl]), pad_tok(kr_s.astype(BF16)), pad_tok(dkb[sl]),
                 pad_tok(dvb[sl]), w_uk[l].T.astype(BF16), wuv, lam_vecs, gsub,
                 cache_mla_ckv, cache_mla_krope, cache_diff_k.reshape(depth, n_pool, PAGE, 512),
                 cache_diff_v.reshape(depth, n_pool, PAGE, 512), layer=l, n_group=n_group, lam_init=lam_init)

    n_tok = b * s_len + n_s
    x_tok = jnp.concatenate([x_prompt.reshape(b * s_len, D_MODEL), x_sample.reshape(n_s, D_MODEL)], axis=0)
    o_tok = jnp.concatenate([o_p.reshape(b, t_pad, D_MODEL)[:, N_META:t].reshape(b * s_len, D_MODEL),
                             o_s.reshape(n_s, D_MODEL).astype(BF16)], axis=0)
    qd2 = sub_keys.shape[-1]
    keys = sub_keys[l].reshape(PEER_HEADS * 2, N_KEYS, qd2)
    blk_eye = jnp.asarray(np.eye(PEER_HEADS * 2), F32)
    kbdt = (keys[:, :, None, :] * blk_eye[:, None, :, None]).reshape(PEER_HEADS * 2 * N_KEYS, PEER_HEADS * 2 * qd2)
    h1, hn, st = _tail(x_tok, o_tok, w_o[l].astype(BF16), g_ffn[l].astype(F32)[None, :], w_pq[l].astype(BF16),
                       kbdt.astype(BF16), _tile(n_tok, 256))

    tn = _tile(n_tok, 512)
    tp, ci, ej = _thr(st, _tile(n_tok, 256))
    hk = lambda a: a.reshape(PEER_HEADS, N_KEYS, n_tok)
    y_t = _peer(hn.T, expert_u[l].astype(BF16), expert_v[l].T.astype(BF16), hk(tp), hk(ci),
                st.reshape(PEER_HEADS, 2, N_KEYS, n_tok), hk(ej), h1.T, tn=tn, n_i=4)
    y = y_t.T
    y_prompt = y[:b * s_len].reshape(b, s_len, D_MODEL)
    y_sample = y[b * s_len:].reshape(db, ds, D_MODEL)

    def prompt_rows(a, shape):
        return a[:n_p].reshape((b, t_pad) + shape)[:, :t][None]

    def sample_rows(a, shape):
        return a[sl].reshape((db, ds) + shape)[None]

    krope_all = kr[:, MLA_NOPE:MLA_NOPE + MLA_ROPE]
    return (y_prompt, y_sample,
            prompt_rows(ckv, (MLA_KV_RANK,)), prompt_rows(krope_all, (MLA_ROPE,)),
            prompt_rows(dk, (DIFF_HEADS, 2 * DIFF_HD)), prompt_rows(dv, (DIFF_HEADS, DIFF_VD)),
            sample_rows(ckv, (MLA_KV_RANK,)), sample_rows(krope_all, (MLA_ROPE,)),
            sample_rows(dk, (DIFF_HEADS, 2 * DIFF_HD)), sample_rows(dv, (DIFF_HEADS, DIFF_VD)))
```

```python
import functools
import math

import numpy as np
import jax
import jax.numpy as jnp
from jax import lax
from jax.experimental import pallas as pl
from jax.experimental.pallas import tpu as pltpu

F32 = jnp.float32
BF16 = jnp.bfloat16

D_MODEL = 1024
N_META = 16
ROPE_THETA = 10000.0
EPS = 1e-6
MLA_HEADS = 8
MLA_NOPE = 64
MLA_ROPE = 32
MLA_VDIM = 64
MLA_Q_RANK = 256
MLA_KV_RANK = 128
MLA_SCALE = (MLA_NOPE + MLA_ROPE) ** -0.5
DIFF_HEADS = 4
DIFF_HD = 64
DIFF_VD = 128
DIFF_SCALE = DIFF_HD ** -0.5
PEER_HEADS = 8
N_KEYS = 128
PEER_TOPK = 16
PAGE = 128
LANES = 128
N_ATTN = MLA_HEADS + 2 * DIFF_HEADS
NEG = -1e30
VMEM_LIMIT = 56 * 1024 * 1024

_NT = (((1,), (1,)), ((), ()))


def _round_up(n, m):
    return -(-n // m) * m


def _tile(n, pref):
    t = min(pref, n)
    assert n % t == 0, (n, t)
    return t


def _params(sem):
    return pltpu.CompilerParams(dimension_semantics=sem, vmem_limit_bytes=VMEM_LIMIT)


def _full(shape):
    nd = len(shape)
    return pl.BlockSpec(shape, lambda *_: (0,) * nd)


def _group_mean(xx, s):
    hi = xx.astype(BF16)
    lo = (xx - hi.astype(F32)).astype(BF16)
    return jnp.dot(hi, s, preferred_element_type=F32) + jnp.dot(lo, s, preferred_element_type=F32)


def _proj_kernel(x_ref, cm_ref, sm_ref, cd_ref, sd_ref, gattn_ref, win_ref, gcq_ref, wuq_ref, gckv_ref,
                 wuk_ref, smla_ref, sdiff_ref, gq_ref, gk_ref, gkr_ref, gdq_ref, gdk_ref,
                 q_ref, km_ref, ckv_ref, ckvb_ref, kr_ref, dk_ref, dkb_ref, dv_ref, dvb_ref):
    tm = x_ref.shape[0]
    x = x_ref[...]
    hn = x * lax.rsqrt(jnp.mean(x * x, axis=-1, keepdims=True) + EPS) * gattn_ref[...]
    z = jnp.dot(hn.astype(BF16), win_ref[...], preferred_element_type=F32)
    lane = lax.broadcasted_iota(jnp.int32, (tm, LANES), 1)
    cm, sm, cd, sd = cm_ref[...], sm_ref[...], cd_ref[...], sd_ref[...]
    smla, sdiff = smla_ref[...], sdiff_ref[...]
    mla_lo = (lane >= MLA_NOPE) & (lane < MLA_NOPE + MLA_ROPE // 2)
    diff_lo = (lane % DIFF_HD) < DIFF_HD // 2
    first_half = lane < DIFF_HD

    def rope_mla(v):
        sw = jnp.where(mla_lo, pltpu.roll(v, LANES - MLA_ROPE // 2, 1), pltpu.roll(v, MLA_ROPE // 2, 1))
        return v * cm + sw * sm

    def rope_diff(v):
        sw = jnp.where(diff_lo, pltpu.roll(v, LANES - DIFF_HD // 2, 1), pltpu.roll(v, DIFF_HD // 2, 1))
        return v * cd + sw * sd

    def group_norm(v, s, g):
        return v * lax.rsqrt(_group_mean(v * v, s) + EPS) * g

    cq = z[:, :MLA_Q_RANK]
    cqn = cq * lax.rsqrt(jnp.mean(cq * cq, axis=-1, keepdims=True) + EPS) * gcq_ref[...]
    q = jnp.dot(cqn.astype(BF16), wuq_ref[...], preferred_element_type=F32)
    for h in range(MLA_HEADS):
        blk = group_norm(q[:, h * LANES:(h + 1) * LANES], smla, gq_ref[...])
        q_ref[:, h * LANES:(h + 1) * LANES] = (rope_mla(blk) * MLA_SCALE).astype(BF16)

    o = MLA_Q_RANK
    ckv = z[:, o:o + MLA_KV_RANK]
    ckvn = ckv * lax.rsqrt(jnp.mean(ckv * ckv, axis=-1, keepdims=True) + EPS) * gckv_ref[...]
    ckv_ref[...] = ckvn
    ckvn_b = ckvn.astype(BF16)
    ckvb_ref[...] = ckvn_b

    o_kr = MLA_Q_RANK + MLA_KV_RANK + 3 * DIFF_HEADS * DIFF_VD
    kr = rope_mla(group_norm(z[:, o_kr:o_kr + LANES], smla, gkr_ref[...]))
    kr_ref[...] = kr

    kpre = jnp.dot(ckvn_b, wuk_ref[...], preferred_element_type=F32)
    for h in range(MLA_HEADS):
        kb = group_norm(kpre[:, h * LANES:(h + 1) * LANES], smla, gk_ref[...])
        km_ref[:, h * LANES:(h + 1) * LANES] = (kb + kr).astype(BF16)

    o_dq = MLA_Q_RANK + MLA_KV_RANK
    o_dk = o_dq + DIFF_HEADS * DIFF_VD
    o_dv = o_dk + DIFF_HEADS * DIFF_VD
    base = MLA_HEADS * LANES
    for h in range(DIFF_HEADS):
        blk = group_norm(z[:, o_dq + h * LANES:o_dq + (h + 1) * LANES], sdiff, gdq_ref[...])
        blk = rope_diff(blk) * DIFF_SCALE
        q_ref[:, base + 2 * h * LANES:base + (2 * h + 1) * LANES] = jnp.where(first_half, blk, 0.0).astype(BF16)
        q_ref[:, base + (2 * h + 1) * LANES:base + (2 * h + 2) * LANES] = jnp.where(first_half, 0.0, blk).astype(BF16)
        kb = rope_diff(group_norm(z[:, o_dk + h * LANES:o_dk + (h + 1) * LANES], sdiff, gdk_ref[...]))
        dk_ref[:, h * LANES:(h + 1) * LANES] = kb
        dkb_ref[:, h * LANES:(h + 1) * LANES] = kb.astype(BF16)
    dv = z[:, o_dv:o_dv + DIFF_HEADS * DIFF_VD]
    dv_ref[...] = dv
    dvb_ref[...] = dv.astype(BF16)


def _proj(rows, tabs, consts, tm):
    r = rows.shape[0]
    row = lambda w: pl.BlockSpec((tm, w), lambda i: (i, 0))
    in_specs = [row(D_MODEL)] + [row(LANES)] * 4 + [_full(c.shape) for c in consts]
    widths = (N_ATTN * LANES, MLA_HEADS * LANES, LANES, LANES, LANES, 512, 512, 512, 512)
    dts = (BF16, BF16, F32, BF16, F32, F32, BF16, F32, BF16)
    return pl.pallas_call(
        _proj_kernel,
        grid=(r // tm,),
        in_specs=in_specs,
        out_specs=[row(w) for w in widths],
        out_shape=[jax.ShapeDtypeStruct((r, w), d) for w, d in zip(widths, dts)],
        compiler_params=_params(("parallel",)),
        name="proj",
    )(rows, *tabs, *consts)


def _lam_value(lam_ref, lam_init):
    lv = lam_ref[...]
    a = jnp.sum(lv[0:1] * lv[1:2], axis=-1, keepdims=True)
    b = jnp.sum(lv[2:3] * lv[3:4], axis=-1, keepdims=True)
    return jnp.exp(a) - jnp.exp(b) + lam_init


def _pattn_kernel(qi_ref, ki_ref, q_ref, km_ref, ckvb_ref, dkb_ref, dvb_ref, wuv_ref, lam_ref, gsub_ref,
                  o_ref, m_scr, l_scr, acc_scr, *, lam_init):
    s = pl.program_id(1)
    qi, ki = qi_ref[s], ki_ref[s]
    tq, tk = q_ref.shape[0], km_ref.shape[0]

    @pl.when(ki == 0)
    def _():
        m_scr[...] = jnp.full(m_scr.shape, NEG, F32)
        l_scr[...] = jnp.zeros(l_scr.shape, F32)
        acc_scr[...] = jnp.zeros(acc_scr.shape, F32)

    def step(masked):
        if masked:
            keep = (lax.broadcasted_iota(jnp.int32, (tq, tk), 0) >= lax.broadcasted_iota(jnp.int32, (tq, tk), 1))
        for hh in range(N_ATTN):
            q = q_ref[:, hh * LANES:(hh + 1) * LANES]
            if hh < MLA_HEADS:
                k = km_ref[:, hh * LANES:(hh + 1) * LANES]
                v = ckvb_ref[...]
            else:
                hd = (hh - MLA_HEADS) // 2
                k = dkb_ref[:, hd * LANES:(hd + 1) * LANES]
                v = dvb_ref[:, hd * LANES:(hd + 1) * LANES]
            sc = lax.dot_general(q, k, _NT, preferred_element_type=F32)
            if masked:
                sc = jnp.where(keep, sc, NEG)
            m_prev = m_scr[hh]
            m_new = jnp.maximum(m_prev, jnp.max(sc, axis=1, keepdims=True))
            alpha = jnp.exp(m_prev - m_new)
            p = jnp.exp(sc - jnp.concatenate([m_new] * (tk // LANES), axis=1))
            l_scr[hh] = alpha * l_scr[hh] + jnp.sum(p, axis=1, keepdims=True)
            acc_scr[hh] = alpha * acc_scr[hh] + jnp.dot(p.astype(BF16), v, preferred_element_type=F32)
            m_scr[hh] = m_new

    @pl.when(ki < qi)
    def _():
        step(False)

    @pl.when(ki == qi)
    def _():
        step(True)
        o_mla = jnp.zeros((tq, MLA_HEADS * MLA_VDIM), F32)
        for h in range(MLA_HEADS):
            a = (acc_scr[h] / l_scr[h]).astype(BF16)
            o_mla = o_mla + jnp.dot(a, wuv_ref[h], preferred_element_type=F32)
        o_ref[:, :MLA_HEADS * MLA_VDIM] = o_mla.astype(o_ref.dtype)
        lam = _lam_value(lam_ref, lam_init)
        for hd in range(DIFF_HEADS):
            h0 = MLA_HEADS + 2 * hd
            od = acc_scr[h0] / l_scr[h0] - lam * (acc_scr[h0 + 1] / l_scr[h0 + 1])
            od = od * lax.rsqrt(jnp.mean(od * od, axis=-1, keepdims=True) + EPS) * gsub_ref[...] * (1.0 - lam_init)
            c0 = MLA_HEADS * MLA_VDIM + hd * DIFF_VD
            o_ref[:, c0:c0 + DIFF_VD] = od.astype(o_ref.dtype)


def _pattn(q_all, km, ckvb, dkb, dvb, wuv_p, lam_vecs, gsub, *, batch, t_pad, tq, lam_init):
    nq = t_pad // tq
    pairs = [(i, j) for i in range(nq) for j in range(i + 1)]
    qi_tab = jnp.asarray([p[0] for p in pairs], jnp.int32)
    ki_tab = jnp.asarray([p[1] for p in pairs], jnp.int32)
    qmap = lambda b, s, qi, ki: (b * nq + qi[s], 0)
    kmap = lambda b, s, qi, ki: (b * nq + ki[s], 0)
    cmap = lambda nd: (lambda b, s, qi, ki: (0,) * nd)
    grid_spec = pltpu.PrefetchScalarGridSpec(
        num_scalar_prefetch=2,
        grid=(batch, len(pairs)),
        in_specs=[
            pl.BlockSpec((tq, N_ATTN * LANES), qmap),
            pl.BlockSpec((tq, MLA_HEADS * LANES), kmap),
            pl.BlockSpec((tq, LANES), kmap),
            pl.BlockSpec((tq, DIFF_HEADS * LANES), kmap),
            pl.BlockSpec((tq, DIFF_HEADS * LANES), kmap),
            pl.BlockSpec(wuv_p.shape, cmap(3)),
            pl.BlockSpec(lam_vecs.shape, cmap(2)),
            pl.BlockSpec(gsub.shape, cmap(2)),
        ],
        out_specs=pl.BlockSpec((tq, D_MODEL), qmap),
        scratch_shapes=[pltpu.VMEM((N_ATTN, tq, LANES), F32)] * 3,
    )
    return pl.pallas_call(
        functools.partial(_pattn_kernel, lam_init=lam_init),
        grid_spec=grid_spec,
        out_shape=jax.ShapeDtypeStruct((batch * t_pad, D_MODEL), BF16),
        compiler_params=_params(("parallel", "arbitrary")),
        name="pattn",
    )(qi_tab, ki_tab, q_all, km, ckvb, dkb, dvb, wuv_p, lam_vecs, gsub)


def _qabs_kernel(q_ref, w_ref, o_ref):
    for h in range(MLA_HEADS):
        o_ref[:, h * LANES:(h + 1) * LANES] = jnp.dot(
            q_ref[:, h * LANES:(h + 1) * LANES], w_ref[h], preferred_element_type=F32).astype(o_ref.dtype)


def _qabs(q_mla, wabs):
    return pl.pallas_call(
        _qabs_kernel,
        out_shape=jax.ShapeDtypeStruct(q_mla.shape, BF16),
        name="qabs",
    )(q_mla, wabs)


def _sattn_kernel(pt_ref, qabs_ref, qrope_ref, qd_ref, sckv_ref, skrt_ref, sdk_ref, sdv_ref, wukt_ref, wuv_ref,
                  lam_ref, gsub_ref, *rest, n_group, dec_seq, lam_init):
    pages = rest[:4 * n_group]
    o_ref, mm_scr, lm_scr, md_scr, ld_scr, accm_scr, accd_scr = rest[4 * n_group:]
    s = pl.program_id(1)
    nq = dec_seq * MLA_HEADS
    qabs, qrope, qd = qabs_ref[0], qrope_ref[0], qd_ref[0]
    n_virt = PAGE * DIFF_HEADS
    own_head = ((lax.broadcasted_iota(jnp.int32, (nq, n_virt), 1) % DIFF_HEADS)
                == (lax.broadcasted_iota(jnp.int32, (nq, n_virt), 0) % MLA_HEADS) // 2)

    def scores(ckv_b, krt_b, dk_b):
        kt = lax.dot_general(wukt_ref[...], ckv_b, _NT, preferred_element_type=F32)
        ms = jnp.sum((kt * kt).reshape(MLA_HEADS, MLA_NOPE, kt.shape[1]), axis=1) * (1.0 / MLA_NOPE)
        r = lax.rsqrt(ms + EPS)
        sn = lax.dot_general(qabs, ckv_b, _NT, preferred_element_type=F32)
        sr = jnp.dot(qrope, krt_b, preferred_element_type=F32)
        sd = lax.dot_general(qd, dk_b, _NT, preferred_element_type=F32)
        return sn * jnp.concatenate([r] * dec_seq, axis=0) + sr, jnp.where(own_head, sd, NEG)

    def update(pieces):
        mm_prev, md_prev = mm_scr[...], md_scr[...]
        mm_new, md_new = mm_prev, md_prev
        for sm, sd, _, _ in pieces:
            mm_new = jnp.maximum(mm_new, jnp.max(sm, axis=1, keepdims=True))
            md_new = jnp.maximum(md_new, jnp.max(sd, axis=1, keepdims=True))
        am, ad = jnp.exp(mm_prev - mm_new), jnp.exp(md_prev - md_new)
        lm, ld = am * lm_scr[...], ad * ld_scr[...]
        accm, accd = am * accm_scr[...], ad * accd_scr[...]
        for sm, sd, ckv_b, dv_b in pieces:
            pm, pd = jnp.exp(sm - mm_new), jnp.exp(sd - md_new)
            lm = lm + jnp.sum(pm, axis=1, keepdims=True)
            ld = ld + jnp.sum(pd, axis=1, keepdims=True)
            accm = accm + jnp.dot(pm.astype(BF16), ckv_b, preferred_element_type=F32)
            accd = accd + jnp.dot(pd.astype(BF16), dv_b, preferred_element_type=F32)
        mm_scr[...], md_scr[...] = mm_new, md_new
        lm_scr[...], ld_scr[...] = lm, ld
        accm_scr[...], accd_scr[...] = accm, accd

    @pl.when(s == 0)
    def _():
        for ref, val in ((mm_scr, NEG), (md_scr, NEG), (lm_scr, 0.0), (ld_scr, 0.0), (accm_scr, 0.0), (accd_scr, 0.0)):
            ref[...] = jnp.full(ref.shape, val, F32)
        sm, sd = scores(sckv_ref[0], skrt_ref[0], sdk_ref[0])
        qm = lax.broadcasted_iota(jnp.int32, sm.shape, 0) // MLA_HEADS
        qv = lax.broadcasted_iota(jnp.int32, sd.shape, 0) // MLA_HEADS
        sm = jnp.where(lax.broadcasted_iota(jnp.int32, sm.shape, 1) <= qm, sm, NEG)
        sd = jnp.where(lax.broadcasted_iota(jnp.int32, sd.shape, 1) // DIFF_HEADS <= qv, sd, NEG)
        update([(sm, sd, sckv_ref[0], sdv_ref[0])])

    pieces = []
    for g in range(n_group):
        ckv_b = pages[g][...].astype(BF16)
        krt_b = pages[n_group + g][...].astype(BF16)
        dk_b = pages[2 * n_group + g][...].astype(BF16)
        dv_b = pages[3 * n_group + g][...].astype(BF16)
        pieces.append(scores(ckv_b, krt_b, dk_b) + (ckv_b, dv_b))
    update(pieces)

    @pl.when(s == pl.num_programs(1) - 1)
    def _():
        row = lax.broadcasted_iota(jnp.int32, (nq, 512), 0) % MLA_HEADS
        lane = lax.broadcasted_iota(jnp.int32, (nq, 512), 1)
        am = (accm_scr[...] / lm_scr[...]).astype(BF16)
        y = jnp.dot(am, wuv_ref[...], preferred_element_type=F32)
        y = jnp.where(lane // MLA_VDIM == row, y, 0.0)
        o_ref[0, :, :512] = jnp.sum(y.reshape(dec_seq, MLA_HEADS, 512), axis=1)
        lam = _lam_value(lam_ref, lam_init)
        hm = lax.broadcasted_iota(jnp.int32, (nq, DIFF_VD), 0) % MLA_HEADS
        ad = (accd_scr[...] / ld_scr[...]) * jnp.where(hm % 2 == 0, 1.0, -lam)
        for hd in range(DIFF_HEADS):
            blk = jnp.sum(jnp.where(hm // 2 == hd, ad, 0.0).reshape(dec_seq, MLA_HEADS, DIFF_VD), axis=1)
            blk = blk * lax.rsqrt(jnp.mean(blk * blk, axis=-1, keepdims=True) + EPS) * gsub_ref[...] * (1.0 - lam_init)
            o_ref[0, :, 512 + hd * DIFF_VD:512 + (hd + 1) * DIFF_VD] = blk


def _sattn(page_table, qabs, qrope, qd, self_ckv, self_krt, self_dk, self_dv, wukt, wuv, lam_vecs, gsub,
           cache_ckv, cache_krt, cache_dk, cache_dv, *, layer, n_group, lam_init):
    db, n_pages = page_table.shape
    dec_seq = qabs.shape[1] // MLA_HEADS
    nq = dec_seq * MLA_HEADS
    n_virt = PAGE * DIFF_HEADS
    assert n_pages % n_group == 0
    pt_flat = page_table.reshape(-1)
    seq = lambda shape: pl.BlockSpec((1,) + shape, lambda b, s, pt: (b, 0, 0))
    const = lambda shape: pl.BlockSpec(shape, lambda b, s, pt: (0,) * len(shape))

    def page_spec(shape, g):
        return pl.BlockSpec((None, None) + shape, lambda b, s, pt: (layer, pt[b * n_pages + s * n_group + g], 0, 0))

    in_specs = [seq((nq, LANES)), seq((nq, MLA_ROPE)), seq((nq, LANES)),
                seq((PAGE, LANES)), seq((MLA_ROPE, PAGE)), seq((n_virt, LANES)), seq((n_virt, LANES)),
                const(wukt.shape), const(wuv.shape), const(lam_vecs.shape), const(gsub.shape)]
    operands = [qabs, qrope, qd, self_ckv, self_krt, self_dk, self_dv, wukt, wuv, lam_vecs, gsub]
    for arr, shape in ((cache_ckv, (PAGE, LANES)), (cache_krt, (MLA_ROPE, PAGE)),
                       (cache_dk, (n_virt, LANES)), (cache_dv, (n_virt, LANES))):
        for g in range(n_group):
            in_specs.append(page_spec(shape, g))
            operands.append(arr)
    col = pltpu.VMEM((nq, 1), F32)
    grid_spec = pltpu.PrefetchScalarGridSpec(
        num_scalar_prefetch=1,
        grid=(db, n_pages // n_group),
        in_specs=in_specs,
        out_specs=pl.BlockSpec((1, dec_seq, D_MODEL), lambda b, s, pt: (b, 0, 0)),
        scratch_shapes=[col, col, col, col, pltpu.VMEM((nq, LANES), F32), pltpu.VMEM((nq, DIFF_VD), F32)],
    )
    return pl.pallas_call(
        functools.partial(_sattn_kernel, n_group=n_group, dec_seq=dec_seq, lam_init=lam_init),
        grid_spec=grid_spec,
        out_shape=jax.ShapeDtypeStruct((db, dec_seq, D_MODEL), F32),
        compiler_params=_params(("parallel", "arbitrary")),
        name="sattn",
    )(pt_flat, *operands)


def _tail_kernel(x_ref, o_ref, wo_ref, gffn_ref, wpq_ref, kbdt_ref, h1_ref, hn_ref, st_ref):
    h1 = x_ref[...] + jnp.dot(o_ref[...].astype(BF16), wo_ref[...], preferred_element_type=F32)
    h1_ref[...] = h1
    hn = (h1 * lax.rsqrt(jnp.mean(h1 * h1, axis=-1, keepdims=True) + EPS) * gffn_ref[...]).astype(BF16)
    hn_ref[...] = hn
    q = jnp.dot(hn, wpq_ref[...], preferred_element_type=F32).astype(BF16)
    st_ref[...] = lax.dot_general(kbdt_ref[...], q, _NT, preferred_element_type=F32)


def _tail(x, o, wo, gffn, wpq, kbdt, tm):
    n = x.shape[0]
    row = lambda w: pl.BlockSpec((tm, w), lambda i: (i, 0))
    n_sc = kbdt.shape[0]
    return pl.pallas_call(
        _tail_kernel,
        grid=(n // tm,),
        in_specs=[row(D_MODEL), row(D_MODEL), _full(wo.shape), _full(gffn.shape), _full(wpq.shape), _full(kbdt.shape)],
        out_specs=[row(D_MODEL), row(D_MODEL), pl.BlockSpec((n_sc, tm), lambda i: (0, i))],
        out_shape=[jax.ShapeDtypeStruct((n, D_MODEL), F32), jax.ShapeDtypeStruct((n, D_MODEL), BF16),
                   jax.ShapeDtypeStruct((n_sc, n), F32)],
        compiler_params=_params(("parallel",)),
        name="tail",
    )(x, o, wo, gffn, wpq, kbdt)


_CAND_ROWS = [(a, (PEER_TOPK + 1) // (a + 1)) for a in range(PEER_TOPK + 1)]


def _extract_max(x, ridx, n_rows):
    m = jnp.max(x, axis=0, keepdims=True)
    first = jnp.min(jnp.where(x == m, ridx, n_rows), axis=0, keepdims=True)
    return m, jnp.where(ridx == first, -jnp.inf, x)


def _thr_kernel(st_ref, tp_ref, ci_ref, ej_ref):
    tn = st_ref.shape[1]
    ridx = lax.broadcasted_iota(jnp.int32, (N_KEYS, tn), 0)
    n_cand = sum(nb for _, nb in _CAND_ROWS)
    cidx = lax.broadcasted_iota(jnp.int32, (n_cand, tn), 0)
    for h in range(PEER_HEADS):
        si = st_ref[h * 2 * N_KEYS:h * 2 * N_KEYS + N_KEYS, :]
        sj = st_ref[h * 2 * N_KEYS + N_KEYS:(h + 1) * 2 * N_KEYS, :]
        tops = []
        for sc in (si, sj):
            x, vals = sc, []
            for _ in range(PEER_TOPK + 1):
                m, x = _extract_max(x, ridx, N_KEYS)
                vals.append(m)
            tops.append(jnp.concatenate(vals, axis=0))
        ti, tj = tops
        mi, mj = ti[0:1], tj[0:1]
        eti, etj = jnp.exp(ti - mi), jnp.exp(tj - mj)
        cand = jnp.concatenate([ti[a:a + 1] + tj[0:nb] for a, nb in _CAND_ROWS], axis=0)
        ecand = jnp.concatenate([eti[a:a + 1] * etj[0:nb] for a, nb in _CAND_ROWS], axis=0)
        x = cand
        for it in range(PEER_TOPK + 1):
            m, x = _extract_max(x, cidx, n_cand)
            if it == PEER_TOPK - 1:
                v16 = m
        thr = 0.5 * (v16 + m)
        z = jnp.sum(jnp.where(cand >= thr, ecand, 0.0), axis=0, keepdims=True)
        tp_ref[h * N_KEYS:(h + 1) * N_KEYS, :] = thr - si
        ci_ref[h * N_KEYS:(h + 1) * N_KEYS, :] = jnp.exp(si - mi) / z
        ej_ref[h * N_KEYS:(h + 1) * N_KEYS, :] = jnp.exp(sj - mj)


def _thr(st, tn):
    n = st.shape[1]
    rows = PEER_HEADS * N_KEYS
    out = pl.BlockSpec((rows, tn), lambda i: (0, i))
    return pl.pallas_call(
        _thr_kernel,
        grid=(n // tn,),
        in_specs=[pl.BlockSpec((st.shape[0], tn), lambda i: (0, i))],
        out_specs=[out] * 3,
        out_shape=[jax.ShapeDtypeStruct((rows, n), F32)] * 3,
        compiler_params=_params(("parallel",)),
        name="thr",
    )(st)


def _peer_kernel(hnt_ref, u_ref, vt_ref, tp_ref, ci_ref, sj_ref, ej_ref, h1t_ref, y_ref, *, n_i):
    e = pl.program_id(1)
    tn = hnt_ref.shape[1]

    @pl.when(e == 0)
    def _():
        y_ref[...] = h1t_ref[...]

    total = None
    for sb in range(n_i // 2):
        es = slice(sb * 2 * N_KEYS, (sb + 1) * 2 * N_KEYS)
        a = jnp.dot(u_ref[es, :], hnt_ref[...], preferred_element_type=F32)
        act = 0.5 * a * (1.0 + lax.erf(a * (2.0 ** -0.5)))
        rows = [[(tp_ref[h, pl.ds(e * n_i + sb * 2 + k, 1), :], ci_ref[h, pl.ds(e * n_i + sb * 2 + k, 1), :])
                 for h in range(PEER_HEADS)] for k in range(2)]
        cols = []
        for lc in range(tn // LANES):
            ls = slice(lc * LANES, (lc + 1) * LANES)
            g = [None, None]
            for h in range(PEER_HEADS):
                sj_t, ej_t = sj_ref[h, 0, :, ls], ej_ref[h, :, ls]
                for k in range(2):
                    term = jnp.where(sj_t >= rows[k][h][0][:, ls], ej_t, 0.0) * rows[k][h][1][:, ls]
                    g[k] = term if g[k] is None else g[k] + term
            cols.append(jnp.concatenate(
                [(act[k * N_KEYS:(k + 1) * N_KEYS, ls] * g[k]).astype(BF16) for k in range(2)], axis=0))
        part = jnp.dot(vt_ref[:, es], jnp.concatenate(cols, axis=1), preferred_element_type=F32)
        total = part if total is None else total + part
    y_ref[...] += total


def _peer(hnt, u_b, vt_b, tp, ci, st4, ej, h1t, *, tn, n_i):
    n = hnt.shape[1]
    te = n_i * N_KEYS
    n_exp = u_b.shape[0]
    tok = lambda rows: pl.BlockSpec((rows, tn), lambda t, e: (0, t))
    hk = pl.BlockSpec((PEER_HEADS, N_KEYS, tn), lambda t, e: (0, 0, t))
    sj_spec = pl.BlockSpec((PEER_HEADS, 1, N_KEYS, tn), lambda t, e: (0, 1, 0, t))
    return pl.pallas_call(
        functools.partial(_peer_kernel, n_i=n_i),
        grid=(n // tn, n_exp // te),
        in_specs=[tok(D_MODEL), pl.BlockSpec((te, D_MODEL), lambda t, e: (e, 0)),
                  pl.BlockSpec((D_MODEL, te), lambda t, e: (0, e)), hk, hk, sj_spec, hk, tok(D_MODEL)],
        out_specs=tok(D_MODEL),
        out_shape=jax.ShapeDtypeStruct((D_MODEL, n), F32),
        compiler_params=_params(("parallel", "arbitrary")),
        name="peer",
    )(hnt, u_b, vt_b, tp, ci, st4, ej, h1t)


def _rope_tables(pos):
    posf = pos.astype(F32)[:, None]
    h16 = MLA_ROPE // 2
    inv16 = ROPE_THETA ** (-jnp.arange(h16, dtype=F32) / h16)
    a16 = posf * inv16[None, :]
    c16, s16 = jnp.cos(a16), jnp.sin(a16)
    n = pos.shape[0]
    ones, zeros = jnp.ones((n, MLA_NOPE), F32), jnp.zeros((n, MLA_NOPE), F32)
    pad1, pad0 = jnp.ones((n, LANES - MLA_NOPE - MLA_ROPE), F32), jnp.zeros((n, LANES - MLA_NOPE - MLA_ROPE), F32)
    cm = jnp.concatenate([ones, c16, c16, pad1], axis=1)
    sm = jnp.concatenate([zeros, -s16, s16, pad0], axis=1)
    h32 = DIFF_HD // 2
    inv32 = ROPE_THETA ** (-jnp.arange(h32, dtype=F32) / h32)
    a32 = posf * inv32[None, :]
    c32, s32 = jnp.cos(a32), jnp.sin(a32)
    cd = jnp.concatenate([c32, c32, c32, c32], axis=1)
    sd = jnp.concatenate([-s32, s32, -s32, s32], axis=1)
    return cm, sm, cd, sd


def _head_pad(w, width):
    k = w.shape[0]
    w = w.reshape(k, -1, width)
    return jnp.pad(w, ((0, 0), (0, 0), (0, LANES - width))).reshape(k, -1)


def _layer_consts(l, g_attn, w_in, g_cq, w_uq, g_ckv, w_uk, g_qn, g_qr, g_kn, g_kr, g_dq, g_dk):
    cuts = np.cumsum((MLA_Q_RANK, MLA_KV_RANK, MLA_ROPE, 512, 512, 512))
    wi = w_in[l]
    w_cq, w_ckv, w_kr = wi[:, :cuts[0]], wi[:, cuts[0]:cuts[1]], wi[:, cuts[1]:cuts[2]]
    w_dq, w_dk, w_dv = wi[:, cuts[2]:cuts[3]], wi[:, cuts[3]:cuts[4]], wi[:, cuts[4]:cuts[5]]
    w_kr = jnp.pad(w_kr, ((0, 0), (MLA_NOPE, LANES - MLA_NOPE - MLA_ROPE)))
    win_p = jnp.concatenate([w_cq, w_ckv, w_dq, w_dk, w_dv, w_kr], axis=1).astype(BF16)
    wuq_p = _head_pad(w_uq[l], MLA_NOPE + MLA_ROPE).astype(BF16)
    wuk_p = _head_pad(w_uk[l], MLA_NOPE).astype(BF16)
    smla = np.zeros((LANES, LANES), np.float32)
    smla[:MLA_NOPE, :MLA_NOPE] = 1.0 / MLA_NOPE
    smla[MLA_NOPE:MLA_NOPE + MLA_ROPE, MLA_NOPE:MLA_NOPE + MLA_ROPE] = 1.0 / MLA_ROPE
    sdiff = np.zeros((LANES, LANES), np.float32)
    sdiff[:DIFF_HD, :DIFF_HD] = 1.0 / DIFF_HD
    sdiff[DIFF_HD:, DIFF_HD:] = 1.0 / DIFF_HD
    z = lambda n: jnp.zeros((n,), F32)
    row = lambda v: v.astype(F32)[None, :]
    gq = row(jnp.concatenate([g_qn[l], g_qr[l], z(LANES - MLA_NOPE - MLA_ROPE)]))
    gk = row(jnp.concatenate([g_kn[l], z(LANES - MLA_NOPE)]))
    gkr = row(jnp.concatenate([z(MLA_NOPE), g_kr[l], z(LANES - MLA_NOPE - MLA_ROPE)]))
    gdq = row(jnp.concatenate([g_dq[l], g_dq[l]]))
    gdk = row(jnp.concatenate([g_dk[l], g_dk[l]]))
    return (row(g_attn[l]), win_p, row(g_cq[l]), wuq_p, row(g_ckv[l]), wuk_p,
            jnp.asarray(smla, BF16), jnp.asarray(sdiff, BF16), gq, gk, gkr, gdq, gdk)


def kernel(x_prompt, x_sample, cache_mla_ckv, cache_mla_krope, cache_diff_k, cache_diff_v, page_table, meta_tokens, g_attn, w_in, g_cq, w_uq, g_ckv, w_uk, w_uv, g_qn, g_qr, g_kn, g_kr, g_dq, g_dk, lambda_q1, lambda_k1, lambda_q2, lambda_k2, g_sub, w_o, g_ffn, w_pq, sub_keys, expert_u, expert_v):
    depth = w_in.shape[0]
    assert depth == 1, "one decoder layer is supported"
    l = 0
    lam_init = 0.8 - 0.6 * math.exp(-0.3 * l)
    b, s_len, _ = x_prompt.shape
    db, ds, _ = x_sample.shape
    n_pages = page_table.shape[1]
    t = N_META + s_len
    tq = 256
    t_pad = _round_up(t, tq)
    tm = 256

    meta = jnp.broadcast_to(meta_tokens[None].astype(x_prompt.dtype), (b, N_META, D_MODEL))
    hp = jnp.pad(jnp.concatenate([meta, x_prompt], axis=1), ((0, 0), (0, t_pad - t), (0, 0)))
    n_p, n_s = b * t_pad, db * ds
    r_pad = _round_up(n_p + n_s, tm)
    rows = jnp.concatenate([hp.reshape(n_p, D_MODEL), x_sample.reshape(n_s, D_MODEL),
                            jnp.zeros((r_pad - n_p - n_s, D_MODEL), F32)], axis=0)
    pos = jnp.concatenate([jnp.tile(jnp.arange(t_pad), b), jnp.tile(n_pages * PAGE + jnp.arange(ds), db),
                           jnp.zeros((r_pad - n_p - n_s,), jnp.int32)])
    consts = _layer_consts(l, g_attn, w_in, g_cq, w_uq, g_ckv, w_uk, g_qn, g_qr, g_kn, g_kr, g_dq, g_dk)
    q_all, km, ckv, ckvb, kr, dk, dkb, dv, dvb = _proj(rows, _rope_tables(pos), consts, tm)

    lam_vecs = jnp.stack([lambda_q1[l], lambda_k1[l], lambda_q2[l], lambda_k2[l]]).astype(F32)
    gsub = g_sub[l].astype(F32)[None, :]
    wuv = w_uv[l].astype(BF16)
    head_of_col = np.arange(MLA_HEADS * MLA_VDIM) // MLA_VDIM
    wuv_p = jnp.where(jnp.asarray(head_of_col[None, None, :] == np.arange(MLA_HEADS)[:, None, None]),
                      wuv[None], jnp.zeros((), BF16))

    o_p = _pattn(q_all, km, ckvb, dkb, dvb, wuv_p, lam_vecs, gsub, batch=b, t_pad=t_pad, tq=tq, lam_init=lam_init)

    sl = slice(n_p, n_p + n_s)
    wabs = (w_uk[l].reshape(MLA_KV_RANK, MLA_HEADS, MLA_NOPE) * g_kn[l][None, None, :]).transpose(1, 2, 0)
    wabs = jnp.pad(wabs, ((0, 0), (0, LANES - MLA_NOPE), (0, 0))).astype(BF16)
    q_s = q_all[sl]
    qabs = _qabs(q_s[:, :MLA_HEADS * LANES], wabs).reshape(db, ds * MLA_HEADS, LANES)
    qrope = q_s[:, :MLA_HEADS * LANES].reshape(db, ds * MLA_HEADS, LANES)[:, :, MLA_NOPE:MLA_NOPE + MLA_ROPE]
    qd = q_s[:, MLA_HEADS * LANES:].reshape(db, ds * MLA_HEADS, LANES)
    pad_tok = lambda a: jnp.pad(a.reshape(db, ds, -1), ((0, 0), (0, PAGE - ds), (0, 0)))
    kr_s = kr[sl][:, MLA_NOPE:MLA_NOPE + MLA_ROPE]
    n_pool = cache_diff_k.shape[1]
    n_virt = PAGE * DIFF_HEADS
    n_group = 8 if n_pages % 8 == 0 else 1
    o_s = _sattn(page_table, qabs, qrope, qd, pad_tok(ckvb[sl]), jnp.swapaxes(pad_tok(kr_s.astype(BF16)), 1, 2),
                 pad_tok(dkb[sl]).reshape(db, n_virt, LANES), pad_tok(dvb[sl]).reshape(db, n_virt, LANES),
                 w_uk[l].T.astype(BF16), wuv, lam_vecs, gsub,
                 cache_mla_ckv, jnp.swapaxes(cache_mla_krope, 2, 3), cache_diff_k.reshape(depth, n_pool, n_virt, LANES),
                 cache_diff_v.reshape(depth, n_pool, n_virt, LANES), layer=l, n_group=n_group, lam_init=lam_init)

    n_tok = b * s_len + n_s
    x_tok = jnp.concatenate([x_prompt.reshape(b * s_len, D_MODEL), x_sample.reshape(n_s, D_MODEL)], axis=0)
    o_tok = jnp.concatenate([o_p.reshape(b, t_pad, D_MODEL)[:, N_META:t].reshape(b * s_len, D_MODEL),
                             o_s.reshape(n_s, D_MODEL).astype(BF16)], axis=0)
    qd2 = sub_keys.shape[-1]
    keys = sub_keys[l].reshape(PEER_HEADS * 2, N_KEYS, qd2)
    blk_eye = jnp.asarray(np.eye(PEER_HEADS * 2), F32)
    kbdt = (keys[:, :, None, :] * blk_eye[:, None, :, None]).reshape(PEER_HEADS * 2 * N_KEYS, PEER_HEADS * 2 * qd2)
    h1, hn, st = _tail(x_tok, o_tok, w_o[l].astype(BF16), g_ffn[l].astype(F32)[None, :], w_pq[l].astype(BF16),
                       kbdt.astype(BF16), _tile(n_tok, 256))

    tn = _tile(n_tok, 512)
    tp, ci, ej = _thr(st, _tile(n_tok, 256))
    hk = lambda a: a.reshape(PEER_HEADS, N_KEYS, n_tok)
    y_t = _peer(hn.T, expert_u[l].astype(BF16), expert_v[l].T.astype(BF16), hk(tp), hk(ci),
                st.reshape(PEER_HEADS, 2, N_KEYS, n_tok), hk(ej), h1.T, tn=tn, n_i=8)
    y = y_t.T
    y_prompt = y[:b * s_len].reshape(b, s_len, D_MODEL)
    y_sample = y[b * s_len:].reshape(db, ds, D_MODEL)

    def prompt_rows(a, shape):
        return a[:n_p].reshape((b, t_pad) + shape)[:, :t][None]

    def sample_rows(a, shape):
        return a[sl].reshape((db, ds) + shape)[None]

    krope_all = kr[:, MLA_NOPE:MLA_NOPE + MLA_ROPE]
    return (y_prompt, y_sample,
            prompt_rows(ckv, (MLA_KV_RANK,)), prompt_rows(krope_all, (MLA_ROPE,)),
            prompt_rows(dk, (DIFF_HEADS, 2 * DIFF_HD)), prompt_rows(dv, (DIFF_HEADS, DIFF_VD)),
            sample_rows(ckv, (MLA_KV_RANK,)), sample_rows(krope_all, (MLA_ROPE,)),
            sample_rows(dk, (DIFF_HEADS, 2 * DIFF_HD)), sample_rows(dv, (DIFF_HEADS, DIFF_VD)))
```

```python
import functools
import math

import numpy as np
import jax
import jax.numpy as jnp
from jax import lax
from jax.experimental import pallas as pl
from jax.experimental.pallas import tpu as pltpu

F32 = jnp.float32
BF16 = jnp.bfloat16

D_MODEL = 1024
N_META = 16
ROPE_THETA = 10000.0
EPS = 1e-6
MLA_HEADS = 8
MLA_NOPE = 64
MLA_ROPE = 32
MLA_VDIM = 64
MLA_Q_RANK = 256
MLA_KV_RANK = 128
MLA_SCALE = (MLA_NOPE + MLA_ROPE) ** -0.5
DIFF_HEADS = 4
DIFF_HD = 64
DIFF_VD = 128
DIFF_SCALE = DIFF_HD ** -0.5
PEER_HEADS = 8
N_KEYS = 128
PEER_TOPK = 16
PAGE = 128
LANES = 128
N_ATTN = MLA_HEADS + 2 * DIFF_HEADS
NEG = -1e30
VMEM_LIMIT = 56 * 1024 * 1024

_NT = (((1,), (1,)), ((), ()))


def _round_up(n, m):
    return -(-n // m) * m


def _tile(n, pref):
    t = min(pref, n)
    assert n % t == 0, (n, t)
    return t


def _params(sem):
    return pltpu.CompilerParams(dimension_semantics=sem, vmem_limit_bytes=VMEM_LIMIT)


def _full(shape):
    nd = len(shape)
    return pl.BlockSpec(shape, lambda *_: (0,) * nd)


def _group_mean(xx, s):
    hi = xx.astype(BF16)
    lo = (xx - hi.astype(F32)).astype(BF16)
    return jnp.dot(hi, s, preferred_element_type=F32) + jnp.dot(lo, s, preferred_element_type=F32)


def _proj_kernel(x_ref, cm_ref, sm_ref, cd_ref, sd_ref, gattn_ref, win_ref, gcq_ref, wuq_ref, gckv_ref,
                 wuk_ref, smla_ref, sdiff_ref, gq_ref, gk_ref, gkr_ref, gdq_ref, gdk_ref,
                 q_ref, km_ref, ckv_ref, ckvb_ref, kr_ref, dk_ref, dkb_ref, dv_ref, dvb_ref):
    tm = x_ref.shape[0]
    x = x_ref[...]
    hn = x * lax.rsqrt(jnp.mean(x * x, axis=-1, keepdims=True) + EPS) * gattn_ref[...]
    z = jnp.dot(hn.astype(BF16), win_ref[...], preferred_element_type=F32)
    lane = lax.broadcasted_iota(jnp.int32, (tm, LANES), 1)
    cm, sm, cd, sd = cm_ref[...], sm_ref[...], cd_ref[...], sd_ref[...]
    smla, sdiff = smla_ref[...], sdiff_ref[...]
    mla_lo = (lane >= MLA_NOPE) & (lane < MLA_NOPE + MLA_ROPE // 2)
    diff_lo = (lane % DIFF_HD) < DIFF_HD // 2
    first_half = lane < DIFF_HD

    def rope_mla(v):
        sw = jnp.where(mla_lo, pltpu.roll(v, LANES - MLA_ROPE // 2, 1), pltpu.roll(v, MLA_ROPE // 2, 1))
        return v * cm + sw * sm

    def rope_diff(v):
        sw = jnp.where(diff_lo, pltpu.roll(v, LANES - DIFF_HD // 2, 1), pltpu.roll(v, DIFF_HD // 2, 1))
        return v * cd + sw * sd

    def group_norm(v, s, g):
        return v * lax.rsqrt(_group_mean(v * v, s) + EPS) * g

    cq = z[:, :MLA_Q_RANK]
    cqn = cq * lax.rsqrt(jnp.mean(cq * cq, axis=-1, keepdims=True) + EPS) * gcq_ref[...]
    q = jnp.dot(cqn.astype(BF16), wuq_ref[...], preferred_element_type=F32)
    for h in range(MLA_HEADS):
        blk = group_norm(q[:, h * LANES:(h + 1) * LANES], smla, gq_ref[...])
        q_ref[:, h * LANES:(h + 1) * LANES] = (rope_mla(blk) * MLA_SCALE).astype(BF16)

    o = MLA_Q_RANK
    ckv = z[:, o:o + MLA_KV_RANK]
    ckvn = ckv * lax.rsqrt(jnp.mean(ckv * ckv, axis=-1, keepdims=True) + EPS) * gckv_ref[...]
    ckv_ref[...] = ckvn
    ckvn_b = ckvn.astype(BF16)
    ckvb_ref[...] = ckvn_b

    o_kr = MLA_Q_RANK + MLA_KV_RANK + 3 * DIFF_HEADS * DIFF_VD
    kr = rope_mla(group_norm(z[:, o_kr:o_kr + LANES], smla, gkr_ref[...]))
    kr_ref[...] = kr

    kpre = jnp.dot(ckvn_b, wuk_ref[...], preferred_element_type=F32)
    for h in range(MLA_HEADS):
        kb = group_norm(kpre[:, h * LANES:(h + 1) * LANES], smla, gk_ref[...])
        km_ref[:, h * LANES:(h + 1) * LANES] = (kb + kr).astype(BF16)

    o_dq = MLA_Q_RANK + MLA_KV_RANK
    o_dk = o_dq + DIFF_HEADS * DIFF_VD
    o_dv = o_dk + DIFF_HEADS * DIFF_VD
    base = MLA_HEADS * LANES
    for h in range(DIFF_HEADS):
        blk = group_norm(z[:, o_dq + h * LANES:o_dq + (h + 1) * LANES], sdiff, gdq_ref[...])
        blk = rope_diff(blk) * DIFF_SCALE
        q_ref[:, base + 2 * h * LANES:base + (2 * h + 1) * LANES] = jnp.where(first_half, blk, 0.0).astype(BF16)
        q_ref[:, base + (2 * h + 1) * LANES:base + (2 * h + 2) * LANES] = jnp.where(first_half, 0.0, blk).astype(BF16)
        kb = rope_diff(group_norm(z[:, o_dk + h * LANES:o_dk + (h + 1) * LANES], sdiff, gdk_ref[...]))
        dk_ref[:, h * LANES:(h + 1) * LANES] = kb
        dkb_ref[:, h * LANES:(h + 1) * LANES] = kb.astype(BF16)
    dv = z[:, o_dv:o_dv + DIFF_HEADS * DIFF_VD]
    dv_ref[...] = dv
    dvb_ref[...] = dv.astype(BF16)


def _proj(rows, tabs, consts, tm):
    r = rows.shape[0]
    row = lambda w: pl.BlockSpec((tm, w), lambda i: (i, 0))
    in_specs = [row(D_MODEL)] + [row(LANES)] * 4 + [_full(c.shape) for c in consts]
    widths = (N_ATTN * LANES, MLA_HEADS * LANES, LANES, LANES, LANES, 512, 512, 512, 512)
    dts = (BF16, BF16, F32, BF16, F32, F32, BF16, F32, BF16)
    return pl.pallas_call(
        _proj_kernel,
        grid=(r // tm,),
        in_specs=in_specs,
        out_specs=[row(w) for w in widths],
        out_shape=[jax.ShapeDtypeStruct((r, w), d) for w, d in zip(widths, dts)],
        compiler_params=_params(("parallel",)),
        name="proj",
    )(rows, *tabs, *consts)


def _lam_value(lam_ref, lam_init):
    lv = lam_ref[...]
    a = jnp.sum(lv[0:1] * lv[1:2], axis=-1, keepdims=True)
    b = jnp.sum(lv[2:3] * lv[3:4], axis=-1, keepdims=True)
    return jnp.exp(a) - jnp.exp(b) + lam_init


def _pattn_kernel(qi_ref, ki_ref, q_ref, km_ref, ckvb_ref, dkb_ref, dvb_ref, wuv_ref, lam_ref, gsub_ref,
                  o_ref, m_scr, l_scr, acc_scr, *, lam_init):
    s = pl.program_id(1)
    qi, ki = qi_ref[s], ki_ref[s]
    tq, tk = q_ref.shape[0], km_ref.shape[0]

    @pl.when(ki == 0)
    def _():
        m_scr[...] = jnp.full(m_scr.shape, NEG, F32)
        l_scr[...] = jnp.zeros(l_scr.shape, F32)
        acc_scr[...] = jnp.zeros(acc_scr.shape, F32)

    def step(masked):
        if masked:
            keep = (lax.broadcasted_iota(jnp.int32, (tq, tk), 0) >= lax.broadcasted_iota(jnp.int32, (tq, tk), 1))
        for hh in range(N_ATTN):
            q = q_ref[:, hh * LANES:(hh + 1) * LANES]
            if hh < MLA_HEADS:
                k = km_ref[:, hh * LANES:(hh + 1) * LANES]
                v = ckvb_ref[...]
            else:
                hd = (hh - MLA_HEADS) // 2
                k = dkb_ref[:, hd * LANES:(hd + 1) * LANES]
                v = dvb_ref[:, hd * LANES:(hd + 1) * LANES]
            sc = lax.dot_general(q, k, _NT, preferred_element_type=F32)
            if masked:
                sc = jnp.where(keep, sc, NEG)
            m_prev = m_scr[hh]
            m_new = jnp.maximum(m_prev, jnp.max(sc, axis=1, keepdims=True))
            alpha = jnp.exp(m_prev - m_new)
            p = jnp.exp(sc - jnp.concatenate([m_new] * (tk // LANES), axis=1))
            l_scr[hh] = alpha * l_scr[hh] + jnp.sum(p, axis=1, keepdims=True)
            acc_scr[hh] = alpha * acc_scr[hh] + jnp.dot(p.astype(BF16), v, preferred_element_type=F32)
            m_scr[hh] = m_new

    @pl.when(ki < qi)
    def _():
        step(False)

    @pl.when(ki == qi)
    def _():
        step(True)
        o_mla = jnp.zeros((tq, MLA_HEADS * MLA_VDIM), F32)
        for h in range(MLA_HEADS):
            a = (acc_scr[h] / l_scr[h]).astype(BF16)
            o_mla = o_mla + jnp.dot(a, wuv_ref[h], preferred_element_type=F32)
        o_ref[:, :MLA_HEADS * MLA_VDIM] = o_mla.astype(o_ref.dtype)
        lam = _lam_value(lam_ref, lam_init)
        for hd in range(DIFF_HEADS):
            h0 = MLA_HEADS + 2 * hd
            od = acc_scr[h0] / l_scr[h0] - lam * (acc_scr[h0 + 1] / l_scr[h0 + 1])
            od = od * lax.rsqrt(jnp.mean(od * od, axis=-1, keepdims=True) + EPS) * gsub_ref[...] * (1.0 - lam_init)
            c0 = MLA_HEADS * MLA_VDIM + hd * DIFF_VD
            o_ref[:, c0:c0 + DIFF_VD] = od.astype(o_ref.dtype)


def _pattn(q_all, km, ckvb, dkb, dvb, wuv_p, lam_vecs, gsub, *, batch, t_pad, tq, lam_init):
    nq = t_pad // tq
    pairs = [(i, j) for i in range(nq) for j in range(i + 1)]
    qi_tab = jnp.asarray([p[0] for p in pairs], jnp.int32)
    ki_tab = jnp.asarray([p[1] for p in pairs], jnp.int32)
    qmap = lambda b, s, qi, ki: (b * nq + qi[s], 0)
    kmap = lambda b, s, qi, ki: (b * nq + ki[s], 0)
    cmap = lambda nd: (lambda b, s, qi, ki: (0,) * nd)
    grid_spec = pltpu.PrefetchScalarGridSpec(
        num_scalar_prefetch=2,
        grid=(batch, len(pairs)),
        in_specs=[
            pl.BlockSpec((tq, N_ATTN * LANES), qmap),
            pl.BlockSpec((tq, MLA_HEADS * LANES), kmap),
            pl.BlockSpec((tq, LANES), kmap),
            pl.BlockSpec((tq, DIFF_HEADS * LANES), kmap),
            pl.BlockSpec((tq, DIFF_HEADS * LANES), kmap),
            pl.BlockSpec(wuv_p.shape, cmap(3)),
            pl.BlockSpec(lam_vecs.shape, cmap(2)),
            pl.BlockSpec(gsub.shape, cmap(2)),
        ],
        out_specs=pl.BlockSpec((tq, D_MODEL), qmap),
        scratch_shapes=[pltpu.VMEM((N_ATTN, tq, LANES), F32)] * 3,
    )
    return pl.pallas_call(
        functools.partial(_pattn_kernel, lam_init=lam_init),
        grid_spec=grid_spec,
        out_shape=jax.ShapeDtypeStruct((batch * t_pad, D_MODEL), BF16),
        compiler_params=_params(("parallel", "arbitrary")),
        name="pattn",
    )(qi_tab, ki_tab, q_all, km, ckvb, dkb, dvb, wuv_p, lam_vecs, gsub)


def _qabs_kernel(q_ref, w_ref, o_ref):
    for h in range(MLA_HEADS):
        o_ref[:, h * LANES:(h + 1) * LANES] = jnp.dot(
            q_ref[:, h * LANES:(h + 1) * LANES], w_ref[h], preferred_element_type=F32).astype(o_ref.dtype)


def _qabs(q_mla, wabs):
    return pl.pallas_call(
        _qabs_kernel,
        out_shape=jax.ShapeDtypeStruct(q_mla.shape, BF16),
        name="qabs",
    )(q_mla, wabs)


def _sattn_kernel(pt_ref, qabs_ref, qrope_ref, qd_ref, sckv_ref, skrt_ref, sdk_ref, sdv_ref, wukt_ref, wuv_ref,
                  lam_ref, gsub_ref, *rest, n_group, dec_seq, lam_init):
    pages = rest[:4 * n_group]
    o_ref, mm_scr, lm_scr, md_scr, ld_scr, accm_scr, accd_scr = rest[4 * n_group:]
    s = pl.program_id(1)
    nq = dec_seq * MLA_HEADS
    qabs, qrope, qd = qabs_ref[0], qrope_ref[0], qd_ref[0]
    n_virt = PAGE * DIFF_HEADS
    own_head = ((lax.broadcasted_iota(jnp.int32, (nq, n_virt), 1) % DIFF_HEADS)
                == (lax.broadcasted_iota(jnp.int32, (nq, n_virt), 0) % MLA_HEADS) // 2)

    def scores(ckv_b, krt_b, dk_b):
        kt = lax.dot_general(wukt_ref[...], ckv_b, _NT, preferred_element_type=F32)
        ms = jnp.sum((kt * kt).reshape(MLA_HEADS, MLA_NOPE, kt.shape[1]), axis=1) * (1.0 / MLA_NOPE)
        r = lax.rsqrt(ms + EPS)
        sn = lax.dot_general(qabs, ckv_b, _NT, preferred_element_type=F32)
        sr = jnp.dot(qrope, krt_b, preferred_element_type=F32)
        sd = lax.dot_general(qd, dk_b, _NT, preferred_element_type=F32)
        return sn * jnp.concatenate([r] * dec_seq, axis=0) + sr, jnp.where(own_head, sd, NEG)

    def update(pieces):
        mm_prev, md_prev = mm_scr[...], md_scr[...]
        mm_new, md_new = mm_prev, md_prev
        for sm, sd, _, _ in pieces:
            mm_new = jnp.maximum(mm_new, jnp.max(sm, axis=1, keepdims=True))
            md_new = jnp.maximum(md_new, jnp.max(sd, axis=1, keepdims=True))
        am, ad = jnp.exp(mm_prev - mm_new), jnp.exp(md_prev - md_new)
        lm, ld = am * lm_scr[...], ad * ld_scr[...]
        accm, accd = am * accm_scr[...], ad * accd_scr[...]
        for sm, sd, ckv_b, dv_b in pieces:
            pm, pd = jnp.exp(sm - mm_new), jnp.exp(sd - md_new)
            lm = lm + jnp.sum(pm, axis=1, keepdims=True)
            ld = ld + jnp.sum(pd, axis=1, keepdims=True)
            accm = accm + jnp.dot(pm.astype(BF16), ckv_b, preferred_element_type=F32)
            accd = accd + jnp.dot(pd.astype(BF16), dv_b, preferred_element_type=F32)
        mm_scr[...], md_scr[...] = mm_new, md_new
        lm_scr[...], ld_scr[...] = lm, ld
        accm_scr[...], accd_scr[...] = accm, accd

    @pl.when(s == 0)
    def _():
        for ref, val in ((mm_scr, NEG), (md_scr, NEG), (lm_scr, 0.0), (ld_scr, 0.0), (accm_scr, 0.0), (accd_scr, 0.0)):
            ref[...] = jnp.full(ref.shape, val, F32)
        sm, sd = scores(sckv_ref[0], skrt_ref[0], sdk_ref[0])
        qm = lax.broadcasted_iota(jnp.int32, sm.shape, 0) // MLA_HEADS
        qv = lax.broadcasted_iota(jnp.int32, sd.shape, 0) // MLA_HEADS
        sm = jnp.where(lax.broadcasted_iota(jnp.int32, sm.shape, 1) <= qm, sm, NEG)
        sd = jnp.where(lax.broadcasted_iota(jnp.int32, sd.shape, 1) // DIFF_HEADS <= qv, sd, NEG)
        update([(sm, sd, sckv_ref[0], sdv_ref[0])])

    pieces = []
    for g in range(n_group):
        ckv_b = pages[g][...].astype(BF16)
        krt_b = pages[n_group + g][...].astype(BF16)
        dk_b = pages[2 * n_group + g][...].astype(BF16)
        dv_b = pages[3 * n_group + g][...].astype(BF16)
        pieces.append(scores(ckv_b, krt_b, dk_b) + (ckv_b, dv_b))
    update(pieces)

    @pl.when(s == pl.num_programs(1) - 1)
    def _():
        row = lax.broadcasted_iota(jnp.int32, (nq, 512), 0) % MLA_HEADS
        lane = lax.broadcasted_iota(jnp.int32, (nq, 512), 1)
        am = (accm_scr[...] / lm_scr[...]).astype(BF16)
        y = jnp.dot(am, wuv_ref[...], preferred_element_type=F32)
        y = jnp.where(lane // MLA_VDIM == row, y, 0.0)
        o_ref[0, :, :512] = jnp.sum(y.reshape(dec_seq, MLA_HEADS, 512), axis=1)
        lam = _lam_value(lam_ref, lam_init)
        hm = lax.broadcasted_iota(jnp.int32, (nq, DIFF_VD), 0) % MLA_HEADS
        ad = (accd_scr[...] / ld_scr[...]) * jnp.where(hm % 2 == 0, 1.0, -lam)
        for hd in range(DIFF_HEADS):
            blk = jnp.sum(jnp.where(hm // 2 == hd, ad, 0.0).reshape(dec_seq, MLA_HEADS, DIFF_VD), axis=1)
            blk = blk * lax.rsqrt(jnp.mean(blk * blk, axis=-1, keepdims=True) + EPS) * gsub_ref[...] * (1.0 - lam_init)
            o_ref[0, :, 512 + hd * DIFF_VD:512 + (hd + 1) * DIFF_VD] = blk


def _sattn(page_table, qabs, qrope, qd, self_ckv, self_krt, self_dk, self_dv, wukt, wuv, lam_vecs, gsub,
           cache_ckv, cache_krt, cache_dk, cache_dv, *, layer, n_group, lam_init):
    db, n_pages = page_table.shape
    dec_seq = qabs.shape[1] // MLA_HEADS
    nq = dec_seq * MLA_HEADS
    n_virt = PAGE * DIFF_HEADS
    assert n_pages % n_group == 0
    pt_flat = page_table.reshape(-1)
    seq = lambda shape: pl.BlockSpec((1,) + shape, lambda b, s, pt: (b, 0, 0))
    const = lambda shape: pl.BlockSpec(shape, lambda b, s, pt: (0,) * len(shape))

    def page_spec(shape, g):
        return pl.BlockSpec((None, None) + shape, lambda b, s, pt: (layer, pt[b * n_pages + s * n_group + g], 0, 0))

    in_specs = [seq((nq, LANES)), seq((nq, MLA_ROPE)), seq((nq, LANES)),
                seq((PAGE, LANES)), seq((MLA_ROPE, PAGE)), seq((n_virt, LANES)), seq((n_virt, LANES)),
                const(wukt.shape), const(wuv.shape), const(lam_vecs.shape), const(gsub.shape)]
    operands = [qabs, qrope, qd, self_ckv, self_krt, self_dk, self_dv, wukt, wuv, lam_vecs, gsub]
    for arr, shape in ((cache_ckv, (PAGE, LANES)), (cache_krt, (MLA_ROPE, PAGE)),
                       (cache_dk, (n_virt, LANES)), (cache_dv, (n_virt, LANES))):
        for g in range(n_group):
            in_specs.append(page_spec(shape, g))
            operands.append(arr)
    col = pltpu.VMEM((nq, 1), F32)
    grid_spec = pltpu.PrefetchScalarGridSpec(
        num_scalar_prefetch=1,
        grid=(db, n_pages // n_group),
        in_specs=in_specs,
        out_specs=pl.BlockSpec((1, dec_seq, D_MODEL), lambda b, s, pt: (b, 0, 0)),
        scratch_shapes=[col, col, col, col, pltpu.VMEM((nq, LANES), F32), pltpu.VMEM((nq, DIFF_VD), F32)],
    )
    return pl.pallas_call(
        functools.partial(_sattn_kernel, n_group=n_group, dec_seq=dec_seq, lam_init=lam_init),
        grid_spec=grid_spec,
        out_shape=jax.ShapeDtypeStruct((db, dec_seq, D_MODEL), F32),
        compiler_params=_params(("parallel", "arbitrary")),
        name="sattn",
    )(pt_flat, *operands)


def _tail_kernel(x_ref, o_ref, wo_ref, gffn_ref, wpq_ref, kbdt_ref, h1_ref, hn_ref, st_ref):
    h1 = x_ref[...] + jnp.dot(o_ref[...].astype(BF16), wo_ref[...], preferred_element_type=F32)
    h1_ref[...] = h1
    hn = (h1 * lax.rsqrt(jnp.mean(h1 * h1, axis=-1, keepdims=True) + EPS) * gffn_ref[...]).astype(BF16)
    hn_ref[...] = hn
    q = jnp.dot(hn, wpq_ref[...], preferred_element_type=F32).astype(BF16)
    st_ref[...] = lax.dot_general(kbdt_ref[...], q, _NT, preferred_element_type=F32)


def _tail(x, o, wo, gffn, wpq, kbdt, tm):
    n = x.shape[0]
    row = lambda w: pl.BlockSpec((tm, w), lambda i: (i, 0))
    n_sc = kbdt.shape[0]
    return pl.pallas_call(
        _tail_kernel,
        grid=(n // tm,),
        in_specs=[row(D_MODEL), row(D_MODEL), _full(wo.shape), _full(gffn.shape), _full(wpq.shape), _full(kbdt.shape)],
        out_specs=[row(D_MODEL), row(D_MODEL), pl.BlockSpec((n_sc, tm), lambda i: (0, i))],
        out_shape=[jax.ShapeDtypeStruct((n, D_MODEL), F32), jax.ShapeDtypeStruct((n, D_MODEL), BF16),
                   jax.ShapeDtypeStruct((n_sc, n), F32)],
        compiler_params=_params(("parallel",)),
        name="tail",
    )(x, o, wo, gffn, wpq, kbdt)


_CAND_ROWS = [(a, (PEER_TOPK + 1) // (a + 1)) for a in range(PEER_TOPK + 1)]


def _extract_max(x, ridx, n_rows):
    m = jnp.max(x, axis=0, keepdims=True)
    first = jnp.min(jnp.where(x == m, ridx, n_rows), axis=0, keepdims=True)
    return m, jnp.where(ridx == first, -jnp.inf, x)


def _thr_kernel(st_ref, tp_ref, ci_ref, ej_ref):
    tn = st_ref.shape[1]
    ridx = lax.broadcasted_iota(jnp.int32, (N_KEYS, tn), 0)
    n_cand = sum(nb for _, nb in _CAND_ROWS)
    cidx = lax.broadcasted_iota(jnp.int32, (n_cand, tn), 0)
    for h in range(PEER_HEADS):
        si = st_ref[h * 2 * N_KEYS:h * 2 * N_KEYS + N_KEYS, :]
        sj = st_ref[h * 2 * N_KEYS + N_KEYS:(h + 1) * 2 * N_KEYS, :]
        tops = []
        for sc in (si, sj):
            x, vals = sc, []
            for _ in range(PEER_TOPK + 1):
                m, x = _extract_max(x, ridx, N_KEYS)
                vals.append(m)
            tops.append(jnp.concatenate(vals, axis=0))
        ti, tj = tops
        mi, mj = ti[0:1], tj[0:1]
        eti, etj = jnp.exp(ti - mi), jnp.exp(tj - mj)
        cand = jnp.concatenate([ti[a:a + 1] + tj[0:nb] for a, nb in _CAND_ROWS], axis=0)
        ecand = jnp.concatenate([eti[a:a + 1] * etj[0:nb] for a, nb in _CAND_ROWS], axis=0)
        x = cand
        for it in range(PEER_TOPK + 1):
            m, x = _extract_max(x, cidx, n_cand)
            if it == PEER_TOPK - 1:
                v16 = m
        thr = 0.5 * (v16 + m)
        z = jnp.sum(jnp.where(cand >= thr, ecand, 0.0), axis=0, keepdims=True)
        tp_ref[h * N_KEYS:(h + 1) * N_KEYS, :] = thr - si
        ci_ref[h * N_KEYS:(h + 1) * N_KEYS, :] = jnp.exp(si - mi) / z
        ej_ref[h * N_KEYS:(h + 1) * N_KEYS, :] = jnp.exp(sj - mj)


def _thr(st, tn):
    n = st.shape[1]
    rows = PEER_HEADS * N_KEYS
    out = pl.BlockSpec((rows, tn), lambda i: (0, i))
    return pl.pallas_call(
        _thr_kernel,
        grid=(n // tn,),
        in_specs=[pl.BlockSpec((st.shape[0], tn), lambda i: (0, i))],
        out_specs=[out] * 3,
        out_shape=[jax.ShapeDtypeStruct((rows, n), F32)] * 3,
        compiler_params=_params(("parallel",)),
        name="thr",
    )(st)


def _peer_kernel(hnt_ref, u_ref, vt_ref, tp_ref, ci_ref, sj_ref, ej_ref, h1t_ref, y_ref, *, n_i):
    e = pl.program_id(1)
    tn = hnt_ref.shape[1]

    @pl.when(e == 0)
    def _():
        y_ref[...] = h1t_ref[...]

    n_sb = n_i // 2
    sub = lambda sb: slice(sb * 2 * N_KEYS, (sb + 1) * 2 * N_KEYS)
    up = lambda sb: jnp.dot(u_ref[sub(sb), :], hnt_ref[...], preferred_element_type=F32)
    total = None
    pending = None
    a_next = up(0)

    def down(total, pending):
        part = jnp.dot(vt_ref[:, sub(pending[0])], pending[1], preferred_element_type=F32)
        return part if total is None else total + part

    for sb in range(n_sb):
        a = a_next
        if sb + 1 < n_sb:
            a_next = up(sb + 1)
        if pending is not None:
            total = down(total, pending)
        act = 0.5 * a * (1.0 + lax.erf(a * (2.0 ** -0.5)))
        rows = [[(tp_ref[h, pl.ds(e * n_i + sb * 2 + k, 1), :], ci_ref[h, pl.ds(e * n_i + sb * 2 + k, 1), :])
                 for h in range(PEER_HEADS)] for k in range(2)]
        cols = []
        for lc in range(tn // LANES):
            ls = slice(lc * LANES, (lc + 1) * LANES)
            g = [None, None]
            for h in range(PEER_HEADS):
                sj_t, ej_t = sj_ref[h, 0, :, ls], ej_ref[h, :, ls]
                for k in range(2):
                    term = jnp.where(sj_t >= rows[k][h][0][:, ls], ej_t, 0.0) * rows[k][h][1][:, ls]
                    g[k] = term if g[k] is None else g[k] + term
            cols.append(jnp.concatenate(
                [(act[k * N_KEYS:(k + 1) * N_KEYS, ls] * g[k]).astype(BF16) for k in range(2)], axis=0))
        pending = (sb, jnp.concatenate(cols, axis=1))
    y_ref[...] += down(total, pending)


def _peer(hnt, u_b, vt_b, tp, ci, st4, ej, h1t, *, tn, n_i):
    n = hnt.shape[1]
    te = n_i * N_KEYS
    n_exp = u_b.shape[0]
    tok = lambda rows: pl.BlockSpec((rows, tn), lambda t, e: (0, t))
    hk = pl.BlockSpec((PEER_HEADS, N_KEYS, tn), lambda t, e: (0, 0, t))
    sj_spec = pl.BlockSpec((PEER_HEADS, 1, N_KEYS, tn), lambda t, e: (0, 1, 0, t))
    return pl.pallas_call(
        functools.partial(_peer_kernel, n_i=n_i),
        grid=(n // tn, n_exp // te),
        in_specs=[tok(D_MODEL), pl.BlockSpec((te, D_MODEL), lambda t, e: (e, 0)),
                  pl.BlockSpec((D_MODEL, te), lambda t, e: (0, e)), hk, hk, sj_spec, hk, tok(D_MODEL)],
        out_specs=tok(D_MODEL),
        out_shape=jax.ShapeDtypeStruct((D_MODEL, n), F32),
        compiler_params=_params(("parallel", "arbitrary")),
        name="peer",
    )(hnt, u_b, vt_b, tp, ci, st4, ej, h1t)


def _rope_tables(pos):
    posf = pos.astype(F32)[:, None]
    h16 = MLA_ROPE // 2
    inv16 = ROPE_THETA ** (-jnp.arange(h16, dtype=F32) / h16)
    a16 = posf * inv16[None, :]
    c16, s16 = jnp.cos(a16), jnp.sin(a16)
    n = pos.shape[0]
    ones, zeros = jnp.ones((n, MLA_NOPE), F32), jnp.zeros((n, MLA_NOPE), F32)
    pad1, pad0 = jnp.ones((n, LANES - MLA_NOPE - MLA_ROPE), F32), jnp.zeros((n, LANES - MLA_NOPE - MLA_ROPE), F32)
    cm = jnp.concatenate([ones, c16, c16, pad1], axis=1)
    sm = jnp.concatenate([zeros, -s16, s16, pad0], axis=1)
    h32 = DIFF_HD // 2
    inv32 = ROPE_THETA ** (-jnp.arange(h32, dtype=F32) / h32)
    a32 = posf * inv32[None, :]
    c32, s32 = jnp.cos(a32), jnp.sin(a32)
    cd = jnp.concatenate([c32, c32, c32, c32], axis=1)
    sd = jnp.concatenate([-s32, s32, -s32, s32], axis=1)
    return cm, sm, cd, sd


def _head_pad(w, width):
    k = w.shape[0]
    w = w.reshape(k, -1, width)
    return jnp.pad(w, ((0, 0), (0, 0), (0, LANES - width))).reshape(k, -1)


def _layer_consts(l, g_attn, w_in, g_cq, w_uq, g_ckv, w_uk, g_qn, g_qr, g_kn, g_kr, g_dq, g_dk):
    cuts = np.cumsum((MLA_Q_RANK, MLA_KV_RANK, MLA_ROPE, 512, 512, 512))
    wi = w_in[l]
    w_cq, w_ckv, w_kr = wi[:, :cuts[0]], wi[:, cuts[0]:cuts[1]], wi[:, cuts[1]:cuts[2]]
    w_dq, w_dk, w_dv = wi[:, cuts[2]:cuts[3]], wi[:, cuts[3]:cuts[4]], wi[:, cuts[4]:cuts[5]]
    w_kr = jnp.pad(w_kr, ((0, 0), (MLA_NOPE, LANES - MLA_NOPE - MLA_ROPE)))
    win_p = jnp.concatenate([w_cq, w_ckv, w_dq, w_dk, w_dv, w_kr], axis=1).astype(BF16)
    wuq_p = _head_pad(w_uq[l], MLA_NOPE + MLA_ROPE).astype(BF16)
    wuk_p = _head_pad(w_uk[l], MLA_NOPE).astype(BF16)
    smla = np.zeros((LANES, LANES), np.float32)
    smla[:MLA_NOPE, :MLA_NOPE] = 1.0 / MLA_NOPE
    smla[MLA_NOPE:MLA_NOPE + MLA_ROPE, MLA_NOPE:MLA_NOPE + MLA_ROPE] = 1.0 / MLA_ROPE
    sdiff = np.zeros((LANES, LANES), np.float32)
    sdiff[:DIFF_HD, :DIFF_HD] = 1.0 / DIFF_HD
    sdiff[DIFF_HD:, DIFF_HD:] = 1.0 / DIFF_HD
    z = lambda n: jnp.zeros((n,), F32)
    row = lambda v: v.astype(F32)[None, :]
    gq = row(jnp.concatenate([g_qn[l], g_qr[l], z(LANES - MLA_NOPE - MLA_ROPE)]))
    gk = row(jnp.concatenate([g_kn[l], z(LANES - MLA_NOPE)]))
    gkr = row(jnp.concatenate([z(MLA_NOPE), g_kr[l], z(LANES - MLA_NOPE - MLA_ROPE)]))
    gdq = row(jnp.concatenate([g_dq[l], g_dq[l]]))
    gdk = row(jnp.concatenate([g_dk[l], g_dk[l]]))
    return (row(g_attn[l]), win_p, row(g_cq[l]), wuq_p, row(g_ckv[l]), wuk_p,
            jnp.asarray(smla, BF16), jnp.asarray(sdiff, BF16), gq, gk, gkr, gdq, gdk)


def kernel(x_prompt, x_sample, cache_mla_ckv, cache_mla_krope, cache_diff_k, cache_diff_v, page_table, meta_tokens, g_attn, w_in, g_cq, w_uq, g_ckv, w_uk, w_uv, g_qn, g_qr, g_kn, g_kr, g_dq, g_dk, lambda_q1, lambda_k1, lambda_q2, lambda_k2, g_sub, w_o, g_ffn, w_pq, sub_keys, expert_u, expert_v):
    depth = w_in.shape[0]
    assert depth == 1, "one decoder layer is supported"
    l = 0
    lam_init = 0.8 - 0.6 * math.exp(-0.3 * l)
    b, s_len, _ = x_prompt.shape
    db, ds, _ = x_sample.shape
    n_pages = page_table.shape[1]
    t = N_META + s_len
    tq = 256
    t_pad = _round_up(t, tq)
    tm = 256

    meta = jnp.broadcast_to(meta_tokens[None].astype(x_prompt.dtype), (b, N_META, D_MODEL))
    hp = jnp.pad(jnp.concatenate([meta, x_prompt], axis=1), ((0, 0), (0, t_pad - t), (0, 0)))
    n_p, n_s = b * t_pad, db * ds
    r_pad = _round_up(n_p + n_s, tm)
    rows = jnp.concatenate([hp.reshape(n_p, D_MODEL), x_sample.reshape(n_s, D_MODEL),
                            jnp.zeros((r_pad - n_p - n_s, D_MODEL), F32)], axis=0)
    pos = jnp.concatenate([jnp.tile(jnp.arange(t_pad), b), jnp.tile(n_pages * PAGE + jnp.arange(ds), db),
                           jnp.zeros((r_pad - n_p - n_s,), jnp.int32)])
    consts = _layer_consts(l, g_attn, w_in, g_cq, w_uq, g_ckv, w_uk, g_qn, g_qr, g_kn, g_kr, g_dq, g_dk)
    q_all, km, ckv, ckvb, kr, dk, dkb, dv, dvb = _proj(rows, _rope_tables(pos), consts, tm)

    lam_vecs = jnp.stack([lambda_q1[l], lambda_k1[l], lambda_q2[l], lambda_k2[l]]).astype(F32)
    gsub = g_sub[l].astype(F32)[None, :]
    wuv = w_uv[l].astype(BF16)
    head_of_col = np.arange(MLA_HEADS * MLA_VDIM) // MLA_VDIM
    wuv_p = jnp.where(jnp.asarray(head_of_col[None, None, :] == np.arange(MLA_HEADS)[:, None, None]),
                      wuv[None], jnp.zeros((), BF16))

    o_p = _pattn(q_all, km, ckvb, dkb, dvb, wuv_p, lam_vecs, gsub, batch=b, t_pad=t_pad, tq=tq, lam_init=lam_init)

    sl = slice(n_p, n_p + n_s)
    wabs = (w_uk[l].reshape(MLA_KV_RANK, MLA_HEADS, MLA_NOPE) * g_kn[l][None, None, :]).transpose(1, 2, 0)
    wabs = jnp.pad(wabs, ((0, 0), (0, LANES - MLA_NOPE), (0, 0))).astype(BF16)
    q_s = q_all[sl]
    qabs = _qabs(q_s[:, :MLA_HEADS * LANES], wabs).reshape(db, ds * MLA_HEADS, LANES)
    qrope = q_s[:, :MLA_HEADS * LANES].reshape(db, ds * MLA_HEADS, LANES)[:, :, MLA_NOPE:MLA_NOPE + MLA_ROPE]
    qd = q_s[:, MLA_HEADS * LANES:].reshape(db, ds * MLA_HEADS, LANES)
    pad_tok = lambda a: jnp.pad(a.reshape(db, ds, -1), ((0, 0), (0, PAGE - ds), (0, 0)))
    kr_s = kr[sl][:, MLA_NOPE:MLA_NOPE + MLA_ROPE]
    n_pool = cache_diff_k.shape[1]
    n_virt = PAGE * DIFF_HEADS
    n_group = next(g for g in (16, 8, 4, 2, 1) if n_pages % g == 0)
    o_s = _sattn(page_table, qabs, qrope, qd, pad_tok(ckvb[sl]), jnp.swapaxes(pad_tok(kr_s.astype(BF16)), 1, 2),
                 pad_tok(dkb[sl]).reshape(db, n_virt, LANES), pad_tok(dvb[sl]).reshape(db, n_virt, LANES),
                 w_uk[l].T.astype(BF16), wuv, lam_vecs, gsub,
                 cache_mla_ckv, jnp.swapaxes(cache_mla_krope, 2, 3), cache_diff_k.reshape(depth, n_pool, n_virt, LANES),
                 cache_diff_v.reshape(depth, n_pool, n_virt, LANES), layer=l, n_group=n_group, lam_init=lam_init)

    n_tok = b * s_len + n_s
    x_tok = jnp.concatenate([x_prompt.reshape(b * s_len, D_MODEL), x_sample.reshape(n_s, D_MODEL)], axis=0)
    o_tok = jnp.concatenate([o_p.reshape(b, t_pad, D_MODEL)[:, N_META:t].reshape(b * s_len, D_MODEL),
                             o_s.reshape(n_s, D_MODEL).astype(BF16)], axis=0)
    qd2 = sub_keys.shape[-1]
    keys = sub_keys[l].reshape(PEER_HEADS * 2, N_KEYS, qd2)
    blk_eye = jnp.asarray(np.eye(PEER_HEADS * 2), F32)
    kbdt = (keys[:, :, None, :] * blk_eye[:, None, :, None]).reshape(PEER_HEADS * 2 * N_KEYS, PEER_HEADS * 2 * qd2)
    h1, hn, st = _tail(x_tok, o_tok, w_o[l].astype(BF16), g_ffn[l].astype(F32)[None, :], w_pq[l].astype(BF16),
                       kbdt.astype(BF16), _tile(n_tok, 256))

    tn = _tile(n_tok, 512)
    tp, ci, ej = _thr(st, _tile(n_tok, 256))
    hk = lambda a: a.reshape(PEER_HEADS, N_KEYS, n_tok)
    y_t = _peer(hn.T, expert_u[l].astype(BF16), expert_v[l].T.astype(BF16), hk(tp), hk(ci),
                st.reshape(PEER_HEADS, 2, N_KEYS, n_tok), hk(ej), h1.T, tn=tn, n_i=8)
    y = y_t.T
    y_prompt = y[:b * s_len].reshape(b, s_len, D_MODEL)
    y_sample = y[b * s_len:].reshape(db, ds, D_MODEL)

    def prompt_rows(a, shape):
        return a[:n_p].reshape((b, t_pad) + shape)[:, :t][None]

    def sample_rows(a, shape):
        return a[sl].reshape((db, ds) + shape)[None]

    krope_all = kr[:, MLA_NOPE:MLA_NOPE + MLA_ROPE]
    return (y_prompt, y_sample,
            prompt_rows(ckv, (MLA_KV_RANK,)), prompt_rows(krope_all, (MLA_ROPE,)),
            prompt_rows(dk, (DIFF_HEADS, 2 * DIFF_HD)), prompt_rows(dv, (DIFF_HEADS, DIFF_VD)),
            sample_rows(ckv, (MLA_KV_RANK,)), sample_rows(krope_all, (MLA_ROPE,)),
            sample_rows(dk, (DIFF_HEADS, 2 * DIFF_HD)), sample_rows(dv, (DIFF_HEADS, DIFF_VD)))
```
